```python
import math
import jax, jax.numpy as jnp
from jax import lax
import numpy as np

D_MODEL = 1024
BATCH = 4
SEQ = 4096
DEPTH = 2
DEC_BATCH = 32
DEC_SEQ = 1
PAST_LEN = 8192
PAGE_SIZE = 128

HEAD_DIM = 64
N_HEADS = D_MODEL // (2 * HEAD_DIM)
BRANCH_W = N_HEADS * HEAD_DIM
ROT_DIM = HEAD_DIM // 4
ROPE_THETA = 500000.0
MOBA_BLOCK = 256
MOBA_TOPK = 3
MOBA_Q_CHUNK = 32
SB_Q_BLOCK = 128
IN_COLS = 8 * BRANCH_W + 2 * D_MODEL
EPS = 1e-6
NEG_INF = -1e30

kernel_name = 'hybrid_moba_stickbreak_decode_step'


def rms_norm(x, g):
    x32 = x.astype(jnp.float32)
    y = x32 * lax.rsqrt(jnp.mean(x32 * x32, axis=-1, keepdims=True) + EPS)
    return (y * g.astype(jnp.float32)).astype(x.dtype)


def partial_rope(x, pos):
    half = ROT_DIM // 2
    inv = ROPE_THETA ** (-jnp.arange(half, dtype=jnp.float32) * 2.0 / ROT_DIM)
    ang = pos[:, None] * inv[None, :]
    cos = jnp.cos(ang)[None, :, None, :]
    sin = jnp.sin(ang)[None, :, None, :]
    x1 = x[..., :half].astype(jnp.float32)
    x2 = x[..., half:ROT_DIM].astype(jnp.float32)
    rot = jnp.concatenate([x1 * cos - x2 * sin, x2 * cos + x1 * sin], axis=-1).astype(x.dtype)
    return jnp.concatenate([rot, x[..., ROT_DIM:]], axis=-1)


def moba_attend(q, k, v, q_off):
    B, Tq, H, Dh = q.shape
    L = k.shape[1]
    nb = -(-L // MOBA_BLOCK)
    pad_k = nb * MOBA_BLOCK - L
    kb = jnp.pad(k, ((0, 0), (0, pad_k), (0, 0), (0, 0))).reshape(B, nb, MOBA_BLOCK, H, Dh).transpose(0, 3, 1, 2, 4)
    vb = jnp.pad(v, ((0, 0), (0, pad_k), (0, 0), (0, 0))).reshape(B, nb, MOBA_BLOCK, H, Dh).transpose(0, 3, 1, 2, 4)
    kmean = jnp.mean(kb.astype(jnp.float32), axis=3)
    C = min(MOBA_Q_CHUNK, Tq)
    nc = -(-Tq // C)
    qp = jnp.pad(q, ((0, 0), (0, nc * C - Tq), (0, 0), (0, 0)))
    qc = qp.reshape(B, nc, C, H, Dh).transpose(1, 0, 3, 2, 4)
    pos = (q_off + jnp.arange(nc * C, dtype=jnp.int32)).reshape(nc, C)
    topk = min(MOBA_TOPK, nb)
    nsel = topk + 1
    b_ix = jnp.arange(B)[:, None, None]
    h_ix = jnp.arange(H)[None, :, None]
    scale = 1.0 / math.sqrt(Dh)

    def step(args):
        qb, pb = args
        own = jnp.minimum(pb // MOBA_BLOCK, nb - 1)
        gate = jnp.einsum('bhcd,bhnd->bhcn', qb.astype(jnp.float32), kmean)
        fully_past = jnp.arange(nb)[None, :] < own[:, None]
        gate = jnp.where(fully_past, gate, NEG_INF)
        _, sel = lax.top_k(gate, topk)
        own_b = jnp.broadcast_to(own[None, None, :, None], (B, H, C, 1)).astype(sel.dtype)
        blocks = jnp.concatenate([sel, own_b], axis=-1)
        block_ok = jnp.concatenate([sel < own_b, jnp.ones((B, H, C, 1), dtype=bool)], axis=-1)
        flat = blocks.reshape(B, H, C * nsel)
        kg = kb[b_ix, h_ix, flat].reshape(B, H, C, nsel * MOBA_BLOCK, Dh)
        vg = vb[b_ix, h_ix, flat].reshape(B, H, C, nsel * MOBA_BLOCK, Dh)
        kpos = blocks[..., None] * MOBA_BLOCK + jnp.arange(MOBA_BLOCK)
        ok = (block_ok[..., None] & (kpos <= pb[None, None, :, None, None])).reshape(B, H, C, nsel * MOBA_BLOCK)
        s = jnp.einsum('bhcd,bhcmd->bhcm', qb, kg).astype(jnp.float32) * scale
        p = jax.nn.softmax(jnp.where(ok, s, NEG_INF), axis=-1)
        return jnp.einsum('bhcm,bhcmd->bhcd', p.astype(vg.dtype), vg)

    out = lax.map(step, (qc, pos))
    return out.transpose(1, 0, 3, 2, 4).reshape(B, nc * C, H, Dh)[:, :Tq]


def stick_breaking_attend(q, k, v, q_off):
    B, Tq, H, Dh = q.shape
    L = k.shape[1]
    C = min(SB_Q_BLOCK, Tq)
    nc = -(-Tq // C)
    qp = jnp.pad(q, ((0, 0), (0, nc * C - Tq), (0, 0), (0, 0)))
    qc = qp.reshape(B, nc, C, H, Dh).transpose(1, 0, 3, 2, 4)
    pos = (q_off + jnp.arange(nc * C, dtype=jnp.int32)).reshape(nc, C)
    kh = k.transpose(0, 2, 1, 3)
    vh = v.transpose(0, 2, 1, 3)
    kpos = jnp.arange(L, dtype=jnp.int32)
    scale = 1.0 / math.sqrt(Dh)

    def step(args):
        qb, pb = args
        z = jnp.einsum('bhcd,bhld->bhcl', qb, kh).astype(jnp.float32) * scale
        causal = kpos[None, :] < pb[:, None]
        log_1m = jnp.where(causal, jax.nn.log_sigmoid(-z), 0.0)
        after = lax.cumsum(log_1m, axis=3, reverse=True) - log_1m
        a = jnp.where(causal, jnp.exp(jax.nn.log_sigmoid(z) + after), 0.0)
        return jnp.einsum('bhcl,bhld->bhcd', a.astype(vh.dtype), vh)

    out = lax.map(step, (qc, pos))
    return out.transpose(1, 0, 3, 2, 4).reshape(B, nc * C, H, Dh)[:, :Tq]


def mixer_layer(x, q_off, past, norm_g, w_in, b_gate, q_norm_g, k_norm_g, w_branch_a, w_branch_b, w_out):
    B, T, _ = x.shape
    h = rms_norm(x, norm_g)
    u = h @ w_in
    sizes = [BRANCH_W] * 8 + [2 * D_MODEL]
    qa, ka, va, ga, qb, kb, vb, gb, gm = jnp.split(u, np.cumsum(sizes)[:-1].tolist(), axis=-1)
    hs = (B, T, N_HEADS, HEAD_DIM)
    pos = jnp.arange(T, dtype=jnp.float32) + q_off
    qa = partial_rope(rms_norm(qa.reshape(hs), q_norm_g), pos)
    ka = partial_rope(rms_norm(ka.reshape(hs), k_norm_g), pos)
    va = va.reshape(hs)
    qb = qb.reshape(hs)
    kb = kb.reshape(hs)
    vb = vb.reshape(hs)
    if past is None:
        ka_all, va_all, kb_all, vb_all = ka, va, kb, vb
    else:
        ka_all = jnp.concatenate([past[0], ka], axis=1)
        va_all = jnp.concatenate([past[1], va], axis=1)
        kb_all = jnp.concatenate([past[2], kb], axis=1)
        vb_all = jnp.concatenate([past[3], vb], axis=1)
    oa = moba_attend(qa, ka_all, va_all, q_off).reshape(B, T, BRANCH_W) * jax.nn.silu(ga)
    ob = stick_breaking_attend(qb, kb_all, vb_all, q_off).reshape(B, T, BRANCH_W) * jax.nn.silu(gb)
    g = jax.nn.sigmoid(gm + b_gate).reshape(B, T, 2, D_MODEL)
    merged = g[:, :, 0] * (oa @ w_branch_a) + g[:, :, 1] * (ob @ w_branch_b)
    y = x + merged @ w_out
    return y, (ka, va, kb, vb)


def setup_inputs(seed: int = 0) -> dict:
    key = jax.random.key(seed)
    ks = jax.random.split(key, 16)
    f32 = jnp.float32
    n_pages = PAST_LEN // PAGE_SIZE
    n_pool = (DEC_BATCH * n_pages * 5) // 4
    cshape = (DEPTH, n_pool, PAGE_SIZE, N_HEADS, HEAD_DIM)
    perm = jax.random.permutation(ks[6], n_pool)[: DEC_BATCH * n_pages]
    return {
        'x_prompt': jax.random.normal(ks[0], (BATCH, SEQ, D_MODEL), f32),
        'x_sample': jax.random.normal(ks[1], (DEC_BATCH, DEC_SEQ, D_MODEL), f32),
        'cache_moba_k': jax.random.normal(ks[2], cshape, f32),
        'cache_moba_v': jax.random.normal(ks[3], cshape, f32),
        'cache_sb_k': jax.random.normal(ks[4], cshape, f32),
        'cache_sb_v': jax.random.normal(ks[5], cshape, f32),
        'page_table': perm.reshape(DEC_BATCH, n_pages).astype(jnp.int32),
        'norm_g': 1.0 + 0.02 * jax.random.normal(ks[7], (DEPTH, D_MODEL), f32),
        'w_in': jax.random.normal(ks[8], (DEPTH, D_MODEL, IN_COLS), f32) * D_MODEL ** -0.5,
        'b_gate': 0.02 * jax.random.normal(ks[9], (DEPTH, 2 * D_MODEL), f32),
        'q_norm_g': 1.0 + 0.02 * jax.random.normal(ks[10], (DEPTH, HEAD_DIM), f32),
        'k_norm_g': 1.0 + 0.02 * jax.random.normal(ks[11], (DEPTH, HEAD_DIM), f32),
        'w_branch_a': jax.random.normal(ks[12], (DEPTH, BRANCH_W, D_MODEL), f32) * BRANCH_W ** -0.5,
        'w_branch_b': jax.random.normal(ks[13], (DEPTH, BRANCH_W, D_MODEL), f32) * BRANCH_W ** -0.5,
        'w_out': jax.random.normal(ks[14], (DEPTH, D_MODEL, D_MODEL), f32) * D_MODEL ** -0.5,
    }


def reference(x_prompt, x_sample, cache_moba_k, cache_moba_v, cache_sb_k, cache_sb_v, page_table,
              norm_g, w_in, b_gate, q_norm_g, k_norm_g, w_branch_a, w_branch_b, w_out):
    n_db, n_pages = page_table.shape
    past_len = n_pages * cache_moba_k.shape[2]

    def gather_past(cache, l):
        return cache[l][page_table].reshape(n_db, past_len, cache.shape[3], cache.shape[4])

    yp = x_prompt
    ys = x_sample
    mk_p, mv_p, sk_p, sv_p = [], [], [], []
    mk_s, mv_s, sk_s, sv_s = [], [], [], []
    for l in range(DEPTH):
        params = (norm_g[l], w_in[l], b_gate[l], q_norm_g[l], k_norm_g[l], w_branch_a[l], w_branch_b[l], w_out[l])
        yp, rows_p = mixer_layer(yp, 0, None, *params)
        past = (gather_past(cache_moba_k, l), gather_past(cache_moba_v, l),
                gather_past(cache_sb_k, l), gather_past(cache_sb_v, l))
        ys, rows_s = mixer_layer(ys, past_len, past, *params)
        mk_p.append(rows_p[0]); mv_p.append(rows_p[1]); sk_p.append(rows_p[2]); sv_p.append(rows_p[3])
        mk_s.append(rows_s[0]); mv_s.append(rows_s[1]); sk_s.append(rows_s[2]); sv_s.append(rows_s[3])
    moba_k_prompt = jnp.stack(mk_p)
    moba_v_prompt = jnp.stack(mv_p)
    sb_k_prompt = jnp.stack(sk_p)
    sb_v_prompt = jnp.stack(sv_p)
    moba_k_sample = jnp.stack(mk_s)
    moba_v_sample = jnp.stack(mv_s)
    sb_k_sample = jnp.stack(sk_s)
    sb_v_sample = jnp.stack(sv_s)
    return (yp, ys, moba_k_prompt, moba_v_prompt, sb_k_prompt, sb_v_prompt,
            moba_k_sample, moba_v_sample, sb_k_sample, sb_v_sample)
```

```python
import functools
import math

import jax
import jax.numpy as jnp
from jax import lax
from jax.experimental import pallas as pl
from jax.experimental.pallas import tpu as pltpu

F32 = jnp.float32
BF16 = jnp.bfloat16

D_MODEL = 1024
HEAD_DIM = 64
N_HEADS = 8
BRANCH_W = N_HEADS * HEAD_DIM
ROT_DIM = HEAD_DIM // 4
ROPE_THETA = 500000.0
MOBA_BLOCK = 256
MOBA_TOPK = 3
EPS = 1e-6
NEG_INF = -1e30
QK_SCALE = 1.0 / math.sqrt(HEAD_DIM)

LANES = 128
HEADS_PER_TILE = LANES // HEAD_DIM
N_PAIRS = N_HEADS // HEADS_PER_TILE
Q_TILE = MOBA_BLOCK
PROJ_ROWS = 256
MERGE_ROWS = 512
STREAM_PAGES = 8
VMEM_LIMIT = 56 * 1024 * 1024

_NT = (((1,), (1,)), ((), ()))


def _split3(x):
    a = x.astype(BF16)
    r = x - a.astype(F32)
    b = r.astype(BF16)
    c = (r - b.astype(F32)).astype(BF16)
    return a, b, c


def _split2(x):
    a = x.astype(BF16)
    b = (x - a.astype(F32)).astype(BF16)
    return a, b


def _sigmoid(x):
    return 1.0 / (1.0 + jnp.exp(-x))


def _log_sigmoid_pair(z):
    t = jnp.log1p(jnp.exp(-jnp.abs(z)))
    return jnp.minimum(-z, 0.0) - t, jnp.minimum(z, 0.0) - t


def _topk_rank(gate, axis_index):
    rank = jnp.zeros(gate.shape, F32)
    for jp in range(gate.shape[0]):
        row = gate[jp:jp + 1]
        rank = rank + jnp.where(row > gate, 1.0,
                                jnp.where(row == gate, jnp.where(jp < axis_index, 1.0, 0.0), 0.0))
    return rank


def _proj_kernel(prompt, x_ref, ng_ref, w_ref, bg_ref, qg_ref, kg_ref, cos_ref, s1_ref, s2_ref, seg_ref,
                 *outs):
    x = x_ref[...]
    ms = jnp.mean(x * x, axis=-1, keepdims=True)
    h = (x * lax.rsqrt(ms + EPS) * ng_ref[...]).astype(BF16)

    def col(c):
        return jnp.dot(h, w_ref[:, c * BRANCH_W:(c + 1) * BRANCH_W], preferred_element_type=F32)

    def head_norm_rope(u, g):
        ss = jnp.dot((u * u).astype(BF16), seg_ref[...], preferred_element_type=F32)
        y = u * lax.rsqrt(ss * (1.0 / HEAD_DIM) + EPS) * g
        return (y * cos_ref[...] + pltpu.roll(y, BRANCH_W - ROT_DIM // 2, 1) * s1_ref[...]
                + pltpu.roll(y, ROT_DIM // 2, 1) * s2_ref[...])

    def silu(u):
        return u * _sigmoid(u)

    qa = head_norm_rope(col(0), qg_ref[...]) * QK_SCALE
    ka = head_norm_rope(col(1), kg_ref[...])
    va = col(2)
    sga = silu(col(3))
    qb = col(4) * QK_SCALE
    kb = col(5)
    vb = col(6)
    sgb = silu(col(7))
    gm = jnp.dot(h, w_ref[:, 8 * BRANCH_W:], preferred_element_type=F32) + bg_ref[...]
    g = _sigmoid(gm)

    if prompt:
        (qa_ref, kabf_ref, vatbf_ref, qb_ref, kbbf_ref, vbtbf_ref, kat_ref, vat_ref, kbt_ref, vbt_ref,
         sga_ref, sgb_ref, g_ref, kmean_ref) = outs
        qa_ref[...] = qa.astype(BF16)
        qb_ref[...] = qb.astype(BF16)
        kabf_ref[...] = ka.astype(BF16)
        kbbf_ref[...] = kb.astype(BF16)
        va_t, vb_t = va.T, vb.T
        vatbf_ref[0] = va_t.astype(BF16)
        vbtbf_ref[0] = vb_t.astype(BF16)
        kat_ref[0] = ka.T
        vat_ref[0] = va_t
        kbt_ref[0] = kb.T
        vbt_ref[0] = vb_t
        kmean_ref[0] = jnp.mean(ka, axis=0, keepdims=True)
    else:
        qa_ref, ka_ref, va_ref, qb_ref, kb_ref, vb_ref, sga_ref, sgb_ref, g_ref = outs
        qa_ref[...] = qa
        qb_ref[...] = qb
        ka_ref[...] = ka
        va_ref[...] = va
        kb_ref[...] = kb
        vb_ref[...] = vb
    sga_ref[...] = sga
    sgb_ref[...] = sgb
    g_ref[...] = g


def _proj(x, ng, w_bf, bg, qg, kg, cos, s1, s2, seg, *, prompt, seq=None):
    n = x.shape[0]
    tm = PROJ_ROWS if prompt else n
    nt = n // tm
    in_cols = w_bf.shape[1]
    row = lambda i: (i, 0)
    const = lambda i: (0, 0)
    wide = pl.BlockSpec((tm, BRANCH_W), row)
    tab_blocks = cos.shape[0] // tm
    table = pl.BlockSpec((tm, BRANCH_W), lambda i: (i % tab_blocks, 0))
    in_specs = [
        pl.BlockSpec((tm, D_MODEL), row),
        pl.BlockSpec((1, D_MODEL), const),
        pl.BlockSpec((D_MODEL, in_cols), const),
        pl.BlockSpec((1, 2 * D_MODEL), const),
        pl.BlockSpec((1, BRANCH_W), const),
        pl.BlockSpec((1, BRANCH_W), const),
        table, table, table,
        pl.BlockSpec((BRANCH_W, BRANCH_W), const),
    ]
    f32w = jax.ShapeDtypeStruct((n, BRANCH_W), F32)
    bf16w = jax.ShapeDtypeStruct((n, BRANCH_W), BF16)
    gate = jax.ShapeDtypeStruct((n, 2 * D_MODEL), F32)
    gate_spec = pl.BlockSpec((tm, 2 * D_MODEL), row)
    if prompt:
        tiles_per_seq = seq // tm
        vtbf = jax.ShapeDtypeStruct((nt, BRANCH_W, tm), BF16)
        vtbf_spec = pl.BlockSpec((1, BRANCH_W, tm), lambda i: (i, 0, 0))
        kvt = jax.ShapeDtypeStruct((n // seq, BRANCH_W, seq), F32)
        kvt_spec = pl.BlockSpec((1, BRANCH_W, tm), lambda i: (i // tiles_per_seq, 0, i % tiles_per_seq))
        kmean = jax.ShapeDtypeStruct((nt, 1, BRANCH_W), F32)
        kmean_spec = pl.BlockSpec((1, 1, BRANCH_W), lambda i: (i, 0, 0))
        out_shape = (bf16w, bf16w, vtbf, bf16w, bf16w, vtbf, kvt, kvt, kvt, kvt, f32w, f32w, gate, kmean)
        out_specs = (wide, wide, vtbf_spec, wide, wide, vtbf_spec, kvt_spec, kvt_spec, kvt_spec, kvt_spec,
                     wide, wide, gate_spec, kmean_spec)
    else:
        out_shape = (f32w,) * 8 + (gate,)
        out_specs = (wide,) * 8 + (gate_spec,)
    return pl.pallas_call(
        functools.partial(_proj_kernel, prompt),
        grid=(nt,),
        in_specs=in_specs,
        out_specs=out_specs,
        out_shape=out_shape,
        compiler_params=pltpu.CompilerParams(dimension_semantics=("arbitrary",), vmem_limit_bytes=VMEM_LIMIT),
        name="proj_prompt" if prompt else "proj_sample",
    )(x, ng, w_bf, bg, qg, kg, cos, s1, s2, seg)


def _head_masked(q, hh):
    lane = lax.broadcasted_iota(jnp.int32, (1, LANES), 1)
    keep = jnp.where((lane // HEAD_DIM) == hh, 1.0, 0.0)
    return (q.astype(F32) * keep).astype(BF16)


def _moba_kernel(q_ref, k_ref, vt_ref, km_ref, o_ref, sel_ref):
    i = pl.program_id(2)
    nblk = km_ref.shape[1]
    q = q_ref[0]
    km_parts = _split3(km_ref[0])
    blk = lax.broadcasted_iota(jnp.int32, (nblk, Q_TILE), 0)
    past = blk < i
    r = lax.broadcasted_iota(jnp.int32, (Q_TILE, Q_TILE), 0)
    c = lax.broadcasted_iota(jnp.int32, (Q_TILE, Q_TILE), 1)
    causal = r <= c

    k_own = k_ref[0, pl.ds(pl.multiple_of(i * Q_TILE, Q_TILE), Q_TILE), :]
    vt_own = vt_ref[0, i]
    qh, state = [], []
    for hh in range(HEADS_PER_TILE):
        qm = _head_masked(q, hh)
        qh.append(qm)
        gate = sum(lax.dot_general(p, qm, _NT, preferred_element_type=F32) for p in km_parts)
        rank = _topk_rank(jnp.where(past, gate, NEG_INF), blk)
        sel_ref[hh] = jnp.where(past, jnp.where(rank < MOBA_TOPK, 1.0, 0.0), 0.0)
        s = lax.dot_general(k_own, qm, _NT, preferred_element_type=F32)
        s = jnp.where(causal, s, NEG_INF)
        m = jnp.max(s, axis=0, keepdims=True)
        p = jnp.exp(s - m)
        l = jnp.sum(p, axis=0, keepdims=True)
        acc = jnp.dot(vt_own[hh * HEAD_DIM:(hh + 1) * HEAD_DIM, :], p.astype(BF16), preferred_element_type=F32)
        state += [m, l, acc]

    def body(j, carry):
        k_j = k_ref[0, pl.ds(pl.multiple_of(j * Q_TILE, Q_TILE), Q_TILE), :]
        vt_j = vt_ref[0, j]
        out = []
        for hh in range(HEADS_PER_TILE):
            m, l, acc = carry[3 * hh:3 * hh + 3]
            s = lax.dot_general(k_j, qh[hh], _NT, preferred_element_type=F32)
            s = jnp.where(sel_ref[hh, pl.ds(j, 1), :] > 0.0, s, NEG_INF)
            m_new = jnp.maximum(m, jnp.max(s, axis=0, keepdims=True))
            alpha = jnp.exp(m - m_new)
            p = jnp.exp(s - m_new)
            l = alpha * l + jnp.sum(p, axis=0, keepdims=True)
            acc = alpha * acc + jnp.dot(vt_j[hh * HEAD_DIM:(hh + 1) * HEAD_DIM, :], p.astype(BF16),
                                        preferred_element_type=F32)
            out += [m_new, l, acc]
        return tuple(out)

    state = lax.fori_loop(0, i, body, tuple(state))
    out_t = jnp.concatenate([state[3 * hh + 2] / state[3 * hh + 1] for hh in range(HEADS_PER_TILE)], axis=0)
    o_ref[0] = out_t.T


def _moba_prompt(q_bf, k_bf, vt_bf, kmean):
    b, t, _ = q_bf.shape
    nblk = t // Q_TILE
    return pl.pallas_call(
        _moba_kernel,
        grid=(b, N_PAIRS, nblk),
        in_specs=[
            pl.BlockSpec((1, Q_TILE, LANES), lambda bi, p, i: (bi, i, p)),
            pl.BlockSpec((1, t, LANES), lambda bi, p, i: (bi, 0, p)),
            pl.BlockSpec((1, nblk, LANES, Q_TILE), lambda bi, p, i: (bi, 0, p, 0)),
            pl.BlockSpec((1, nblk, LANES), lambda bi, p, i: (bi, 0, p)),
        ],
        out_specs=pl.BlockSpec((1, Q_TILE, LANES), lambda bi, p, i: (bi, i, p)),
        out_shape=jax.ShapeDtypeStruct((b, t, BRANCH_W), F32),
        scratch_shapes=[pltpu.VMEM((HEADS_PER_TILE, nblk, Q_TILE), F32)],
        compiler_params=pltpu.CompilerParams(dimension_semantics=("arbitrary",) * 3, vmem_limit_bytes=VMEM_LIMIT),
        name="moba_prompt",
    )(q_bf, k_bf, vt_bf, kmean)


def _sb_kernel(q_ref, k_ref, vt_ref, u_ref, o_ref):
    i = pl.program_id(2)
    q = q_ref[0]
    u = u_ref[...]
    r = lax.broadcasted_iota(jnp.int32, (Q_TILE, Q_TILE), 0)
    c = lax.broadcasted_iota(jnp.int32, (Q_TILE, Q_TILE), 1)
    strict = r < c
    qh = [_head_masked(q, hh) for hh in range(HEADS_PER_TILE)]

    def block(j, carry, mask):
        k_j = k_ref[0, pl.ds(pl.multiple_of(j * Q_TILE, Q_TILE), Q_TILE), :]
        vt_j = vt_ref[0, j]
        out = []
        for hh in range(HEADS_PER_TILE):
            tail, acc = carry[2 * hh:2 * hh + 2]
            z = lax.dot_general(k_j, qh[hh], _NT, preferred_element_type=F32)
            lm, lp = _log_sigmoid_pair(z)
            if mask is not None:
                lm = jnp.where(mask, lm, 0.0)
            hi, lo = _split2(lm)
            after = (tail + jnp.dot(u, hi, preferred_element_type=F32)
                     + jnp.dot(u, lo, preferred_element_type=F32))
            a = jnp.exp(lp + after)
            if mask is not None:
                a = jnp.where(mask, a, 0.0)
            acc = acc + jnp.dot(vt_j[hh * HEAD_DIM:(hh + 1) * HEAD_DIM, :], a.astype(BF16),
                                preferred_element_type=F32)
            out += [tail + jnp.sum(lm, axis=0, keepdims=True), acc]
        return tuple(out)

    init = (jnp.zeros((1, Q_TILE), F32), jnp.zeros((HEAD_DIM, Q_TILE), F32)) * HEADS_PER_TILE
    state = block(i, init, strict)
    state = lax.fori_loop(0, i, lambda it, carry: block(i - 1 - it, carry, None), state)
    out_t = jnp.concatenate([state[2 * hh + 1] for hh in range(HEADS_PER_TILE)], axis=0)
    o_ref[0] = out_t.T


def _sb_prompt(q_bf, k_bf, vt_bf, upper):
    b, t, _ = q_bf.shape
    nblk = t // Q_TILE
    return pl.pallas_call(
        _sb_kernel,
        grid=(b, N_PAIRS, nblk),
        in_specs=[
            pl.BlockSpec((1, Q_TILE, LANES), lambda bi, p, i: (bi, i, p)),
            pl.BlockSpec((1, t, LANES), lambda bi, p, i: (bi, 0, p)),
            pl.BlockSpec((1, nblk, LANES, Q_TILE), lambda bi, p, i: (bi, 0, p, 0)),
            pl.BlockSpec((Q_TILE, Q_TILE), lambda bi, p, i: (0, 0)),
        ],
        out_specs=pl.BlockSpec((1, Q_TILE, LANES), lambda bi, p, i: (bi, i, p)),
        out_shape=jax.ShapeDtypeStruct((b, t, BRANCH_W), F32),
        compiler_params=pltpu.CompilerParams(dimension_semantics=("arbitrary",) * 3, vmem_limit_bytes=VMEM_LIMIT),
        name="sb_prompt",
    )(q_bf, k_bf, vt_bf, upper)


def _merge_kernel(x_ref, oa_ref, ob_ref, sga_ref, sgb_ref, g_ref, wa_ref, wb_ref, wo_ref, y_ref):
    a = jnp.dot((oa_ref[...] * sga_ref[...]).astype(BF16), wa_ref[...], preferred_element_type=F32)
    b = jnp.dot((ob_ref[...] * sgb_ref[...]).astype(BF16), wb_ref[...], preferred_element_type=F32)
    g = g_ref[...]
    merged = g[:, :D_MODEL] * a + g[:, D_MODEL:] * b
    y_ref[...] = x_ref[...] + jnp.dot(merged.astype(BF16), wo_ref[...], preferred_element_type=F32)


def _merge(x, oa, ob, sga, sgb, g, wa_bf, wb_bf, wo_bf, *, name):
    n = x.shape[0]
    tm = min(MERGE_ROWS, n)
    row = lambda i: (i, 0)
    const = lambda i: (0, 0)
    wide = pl.BlockSpec((tm, BRANCH_W), row)
    return pl.pallas_call(
        _merge_kernel,
        grid=(n // tm,),
        in_specs=[
            pl.BlockSpec((tm, D_MODEL), row), wide, wide, wide, wide,
            pl.BlockSpec((tm, 2 * D_MODEL), row),
            pl.BlockSpec((BRANCH_W, D_MODEL), const),
            pl.BlockSpec((BRANCH_W, D_MODEL), const),
            pl.BlockSpec((D_MODEL, D_MODEL), const),
        ],
        out_specs=pl.BlockSpec((tm, D_MODEL), row),
        out_shape=jax.ShapeDtypeStruct((n, D_MODEL), F32),
        compiler_params=pltpu.CompilerParams(dimension_semantics=("arbitrary",), vmem_limit_bytes=VMEM_LIMIT),
        name=name,
    )(x, oa, ob, sga, sgb, g, wa_bf, wb_bf, wo_bf)


def _stream_kernel(layer, nb, n_pages, pt_ref, qm_ref, qs_ref, mk_hbm, sk_hbm, sv_hbm, w_ref,
                   ob_ref, s_ref, sel_ref, mk_buf, sk_buf, sv_buf, sems, tail_ref, acc_ref):
    b = pl.program_id(0)
    c = pl.program_id(1)
    n_chunks = n_pages // STREAM_PAGES
    total = nb * n_chunks
    step = b * n_chunks + c
    slot = step % 2
    page_len = sk_buf.shape[-1]
    pages_per_block = MOBA_BLOCK // page_len
    n_blocks = n_pages // pages_per_block

    def copies(bb, cc, slot_):
        out = []
        for g in range(STREAM_PAGES):
            page = pt_ref[bb, n_pages - 1 - (cc * STREAM_PAGES + g)]
            for ci, (hbm, buf) in enumerate(((mk_hbm, mk_buf), (sk_hbm, sk_buf), (sv_hbm, sv_buf))):
                out.append(pltpu.make_async_copy(hbm.at[layer, page], buf.at[slot_, g], sems.at[slot_, ci]))
        return out

    @pl.when(step == 0)
    def _():
        for cp in copies(0, 0, 0):
            cp.start()

    @pl.when(step + 1 < total)
    def _():
        nxt = step + 1
        for cp in copies(nxt // n_chunks, nxt % n_chunks, 1 - slot):
            cp.start()

    for cp in copies(b, c, slot):
        cp.wait()

    @pl.when(c == 0)
    def _():
        tail_ref[...] = jnp.zeros_like(tail_ref)
        acc_ref[...] = jnp.zeros_like(acc_ref)

    def page_step(g, tail):
        n = n_pages - 1 - (c * STREAM_PAGES + g)
        s_ref[0, n] = jnp.sum(mk_buf[slot, g] * qm_ref[0], axis=1)
        z = jnp.sum(sk_buf[slot, g] * qs_ref[0], axis=1)
        lm, lp = _log_sigmoid_pair(z)
        hi, lo = _split2(lm)
        after = (tail + jnp.dot(hi, w_ref[...], preferred_element_type=F32)
                 + jnp.dot(lo, w_ref[...], preferred_element_type=F32))
        a = jnp.exp(lp + after)
        acc_ref[...] += a[:, None, :] * sv_buf[slot, g]
        return tail + jnp.sum(lm, axis=-1, keepdims=True)

    tail_ref[...] = lax.fori_loop(0, STREAM_PAGES, page_step, tail_ref[...])

    @pl.when(c == n_chunks - 1)
    def _():
        ob_ref[0] = jnp.sum(acc_ref[...], axis=-1)
        scores = s_ref[0].reshape(n_blocks, pages_per_block, N_HEADS, page_len)
        gate = jnp.sum(jnp.sum(scores, axis=1), axis=-1, keepdims=True) * (1.0 / MOBA_BLOCK)
        gate = jnp.broadcast_to(gate, (n_blocks, N_HEADS, page_len))
        blk = lax.broadcasted_iota(jnp.int32, gate.shape, 0)
        rank = _topk_rank(gate, blk)
        for t in range(MOBA_TOPK):
            sel_ref[0, t] = jnp.sum(jnp.where(rank == float(t), blk.astype(F32), 0.0), axis=0).astype(jnp.int32)


def _decode_stream(layer, page_table, q_moba, q_sb, mk, sk, sv, lower):
    nb, n_pages = page_table.shape
    page_len = mk.shape[-1]
    n_chunks = n_pages // STREAM_PAGES
    any_spec = pl.BlockSpec(memory_space=pl.ANY)
    buf = pltpu.VMEM((2, STREAM_PAGES, N_HEADS, HEAD_DIM, page_len), F32)
    q_spec = pl.BlockSpec((1, N_HEADS, HEAD_DIM, page_len), lambda b, c, pt: (b, 0, 0, 0))
    grid_spec = pltpu.PrefetchScalarGridSpec(
        num_scalar_prefetch=1,
        grid=(nb, n_chunks),
        in_specs=[q_spec, q_spec, any_spec, any_spec, any_spec,
                  pl.BlockSpec((page_len, page_len), lambda b, c, pt: (0, 0))],
        out_specs=(
            pl.BlockSpec((1, N_HEADS, HEAD_DIM), lambda b, c, pt: (b, 0, 0)),
            pl.BlockSpec((1, n_pages, N_HEADS, page_len), lambda b, c, pt: (b, 0, 0, 0)),
            pl.BlockSpec((1, MOBA_TOPK, N_HEADS, page_len), lambda b, c, pt: (b, 0, 0, 0)),
        ),
        scratch_shapes=[buf, buf, buf, pltpu.SemaphoreType.DMA((2, 3)),
                        pltpu.VMEM((N_HEADS, page_len), F32), pltpu.VMEM((N_HEADS, HEAD_DIM, page_len), F32)],
    )
    return pl.pallas_call(
        functools.partial(_stream_kernel, layer, nb, n_pages),
        grid_spec=grid_spec,
        out_shape=(jax.ShapeDtypeStruct((nb, N_HEADS, HEAD_DIM), F32),
                   jax.ShapeDtypeStruct((nb, n_pages, N_HEADS, page_len), F32),
                   jax.ShapeDtypeStruct((nb, MOBA_TOPK, N_HEADS, page_len), jnp.int32)),
        compiler_params=pltpu.CompilerParams(dimension_semantics=("arbitrary", "arbitrary"),
                                             vmem_limit_bytes=VMEM_LIMIT),
        name="decode_stream",
    )(page_table, q_moba, q_sb, mk, sk, sv, lower)


def _moba_decode_kernel(layer, nb, pt_ref, selsm_ref, s_ref, selv_ref, q_ref, kn_ref, vn_ref, mv_hbm, o_ref,
                        vbuf, sems, p_ref):
    b = pl.program_id(0)
    slot = b % 2
    n_pages, _, page_len = s_ref.shape[1:]
    pages_per_block = MOBA_BLOCK // page_len

    def tiles():
        return [(h, t, half) for h in range(N_HEADS) for t in range(MOBA_TOPK) for half in range(pages_per_block)]

    def copies(bb, slot_):
        out = []
        for h, t, half in tiles():
            page = pt_ref[bb, selsm_ref[bb, t * N_HEADS + h] * pages_per_block + half]
            out.append(pltpu.make_async_copy(mv_hbm.at[layer, page, h], vbuf.at[slot_, h, t, half], sems.at[slot_]))
        return out

    @pl.when(b == 0)
    def _():
        for cp in copies(0, 0):
            cp.start()

    @pl.when(b + 1 < nb)
    def _():
        for cp in copies(b + 1, 1 - slot):
            cp.start()

    s = s_ref[0]
    page_blk = lax.broadcasted_iota(jnp.int32, s.shape, 0) // pages_per_block
    sm = jnp.full(s.shape, NEG_INF, F32)
    for t in range(MOBA_TOPK):
        sm = jnp.where(page_blk == selv_ref[0, t][None], s, sm)
    s_new = jnp.sum(q_ref[0] * kn_ref[0], axis=-1, keepdims=True)
    m = jnp.maximum(jnp.max(jnp.max(sm, axis=0), axis=-1, keepdims=True), s_new)
    p = jnp.exp(sm - m)
    p_new = jnp.exp(s_new - m)
    l = jnp.sum(jnp.sum(p, axis=0), axis=-1, keepdims=True) + p_new
    p_ref[...] = p

    for cp in copies(b, slot):
        cp.wait()

    head = lax.broadcasted_iota(jnp.int32, (N_HEADS, HEAD_DIM), 0)
    acc = p_new * vn_ref[0]
    for h, t, half in tiles():
        page = selsm_ref[b, t * N_HEADS + h] * pages_per_block + half
        res = lax.dot_general(p_ref[page].astype(BF16), vbuf[slot, h, t, half].astype(BF16), _NT,
                              preferred_element_type=F32)
        acc = acc + jnp.where(head == h, res, 0.0)
    o_ref[0] = acc / l


def _moba_decode(layer, page_table, sel_scalar, scores, sel_vec, q, k_new, v_new, mv):
    nb, n_pages = page_table.shape
    page_len = mv.shape[-1]
    pages_per_block = MOBA_BLOCK // page_len
    any_spec = pl.BlockSpec(memory_space=pl.ANY)
    row = pl.BlockSpec((1, N_HEADS, HEAD_DIM), lambda b, pt, sl: (b, 0, 0))
    grid_spec = pltpu.PrefetchScalarGridSpec(
        num_scalar_prefetch=2,
        grid=(nb,),
        in_specs=[
            pl.BlockSpec((1, n_pages, N_HEADS, page_len), lambda b, pt, sl: (b, 0, 0, 0)),
            pl.BlockSpec((1, MOBA_TOPK, N_HEADS, page_len), lambda b, pt, sl: (b, 0, 0, 0)),
            row, row, row, any_spec,
        ],
        out_specs=row,
        scratch_shapes=[pltpu.VMEM((2, N_HEADS, MOBA_TOPK, pages_per_block, HEAD_DIM, page_len), F32),
                        pltpu.SemaphoreType.DMA((2,)),
                        pltpu.VMEM((n_pages, N_HEADS, page_len), F32)],
    )
    return pl.pallas_call(
        functools.partial(_moba_decode_kernel, layer, nb),
        grid_spec=grid_spec,
        out_shape=jax.ShapeDtypeStruct((nb, N_HEADS, HEAD_DIM), F32),
        compiler_params=pltpu.CompilerParams(dimension_semantics=("arbitrary",), vmem_limit_bytes=VMEM_LIMIT),
        name="moba_decode",
    )(page_table, sel_scalar, scores, sel_vec, q, k_new, v_new, mv)


def _rope_tables(pos):
    half = ROT_DIM // 2
    inv = ROPE_THETA ** (-jnp.arange(half, dtype=F32) * 2.0 / ROT_DIM)
    ang = pos[:, None] * inv[None, :]
    cos, sin = jnp.cos(ang), jnp.sin(ang)
    n = pos.shape[0]
    pad = jnp.zeros((n, HEAD_DIM - ROT_DIM), F32)
    zero = jnp.zeros((n, half), F32)
    cos_h = jnp.concatenate([cos, cos, pad + 1.0], axis=1)
    s1_h = jnp.concatenate([-sin, zero, pad], axis=1)
    s2_h = jnp.concatenate([zero, sin, pad], axis=1)
    return tuple(jnp.tile(t, (1, N_HEADS)) for t in (cos_h, s1_h, s2_h))


def kernel(x_prompt, x_sample, cache_moba_k, cache_moba_v, cache_sb_k, cache_sb_v, page_table,
           norm_g, w_in, b_gate, q_norm_g, k_norm_g, w_branch_a, w_branch_b, w_out):
    depth = w_in.shape[0]
    bsz, seq, _ = x_prompt.shape
    nb, dec_seq, _ = x_sample.shape
    page_len = cache_moba_k.shape[2]
    n_pages = page_table.shape[1]
    past_len = n_pages * page_len
    n_prompt = bsz * seq
    n_qblk = seq // Q_TILE
    assert dec_seq == 1 and seq % Q_TILE == 0 and MOBA_BLOCK % page_len == 0
    assert n_pages % STREAM_PAGES == 0 and past_len % MOBA_BLOCK == 0 and page_len == LANES

    lane_head = jnp.arange(BRANCH_W) // HEAD_DIM
    seg = (lane_head[:, None] == lane_head[None, :]).astype(BF16)
    idx = jnp.arange(Q_TILE)
    upper_q = (idx[None, :] > idx[:, None]).astype(BF16)
    lower_p = upper_q[:page_len, :page_len].T

    cos_p, s1_p, s2_p = _rope_tables(jnp.arange(seq, dtype=F32))
    cos_s, s1_s, s2_s = _rope_tables(jnp.full((nb,), past_len, F32))

    mk_t, mv_t, sk_t, sv_t = (jnp.transpose(c, (0, 1, 3, 4, 2))
                              for c in (cache_moba_k, cache_moba_v, cache_sb_k, cache_sb_v))

    def lane_replicated(q):
        return jnp.broadcast_to(q.reshape(nb, N_HEADS, HEAD_DIM, 1), (nb, N_HEADS, HEAD_DIM, page_len))

    yp = x_prompt.reshape(n_prompt, D_MODEL)
    ys = x_sample.reshape(nb, D_MODEL)
    rows_p = [[] for _ in range(4)]
    rows_s = [[] for _ in range(4)]
    for l in range(depth):
        w_bf = w_in[l].astype(BF16)
        wa_bf, wb_bf, wo_bf = (w[l].astype(BF16) for w in (w_branch_a, w_branch_b, w_out))
        params = (norm_g[l][None, :], w_bf, b_gate[l][None, :],
                  jnp.tile(q_norm_g[l], N_HEADS)[None, :], jnp.tile(k_norm_g[l], N_HEADS)[None, :])

        (qa, ka_bf, vat_bf, qb, kb_bf, vbt_bf, ka_t, va_t, kb_t, vb_t, sga, sgb, g, kmean) = _proj(
            yp, *params, cos_p, s1_p, s2_p, seg, prompt=True, seq=seq)
        as3 = lambda a: a.reshape(bsz, seq, BRANCH_W)
        vt4 = lambda a: a.reshape(bsz, n_qblk, BRANCH_W, Q_TILE)
        oa = _moba_prompt(as3(qa), as3(ka_bf), vt4(vat_bf), kmean.reshape(bsz, n_qblk, BRANCH_W))
        ob = _sb_prompt(as3(qb), as3(kb_bf), vt4(vbt_bf), upper_q)
        yp = _merge(yp, oa.reshape(n_prompt, BRANCH_W), ob.reshape(n_prompt, BRANCH_W), sga, sgb, g,
                    wa_bf, wb_bf, wo_bf, name="merge_prompt")
        for dst, src in zip(rows_p, (ka_t, va_t, kb_t, vb_t)):
            dst.append(jnp.transpose(src.reshape(bsz, N_HEADS, HEAD_DIM, seq), (0, 3, 1, 2)))

        (qa_s, ka_s, va_s, qb_s, kb_s, vb_s, sga_s, sgb_s, g_s) = _proj(
            ys, *params, cos_s, s1_s, s2_s, seg, prompt=False)
        heads = lambda a: a.reshape(nb, N_HEADS, HEAD_DIM)
        ob_s, scores, sel_vec = _decode_stream(l, page_table, lane_replicated(qa_s), lane_replicated(qb_s),
                                               mk_t, sk_t, sv_t, lower_p)
        sel_scalar = sel_vec[:, :, :, 0].reshape(nb, MOBA_TOPK * N_HEADS)
        oa_s = _moba_decode(l, page_table, sel_scalar, scores, sel_vec, heads(qa_s), heads(ka_s), heads(va_s), mv_t)
        ys = _merge(ys, oa_s.reshape(nb, BRANCH_W), ob_s.reshape(nb, BRANCH_W), sga_s, sgb_s, g_s,
                    wa_bf, wb_bf, wo_bf, name="merge_sample")
        for dst, src in zip(rows_s, (ka_s, va_s, kb_s, vb_s)):
            dst.append(src.reshape(nb, 1, N_HEADS, HEAD_DIM))

    return (yp.reshape(bsz, seq, D_MODEL), ys.reshape(nb, 1, D_MODEL),
            *(jnp.stack(r) for r in rows_p), *(jnp.stack(r) for r in rows_s))
```

```python
import functools
import math

import jax
import jax.numpy as jnp
from jax import lax
from jax.experimental import pallas as pl
from jax.experimental.pallas import tpu as pltpu

F32 = jnp.float32
BF16 = jnp.bfloat16

D_MODEL = 1024
HEAD_DIM = 64
N_HEADS = 8
BRANCH_W = N_HEADS * HEAD_DIM
ROT_DIM = HEAD_DIM // 4
ROPE_THETA = 500000.0
MOBA_BLOCK = 256
MOBA_TOPK = 3
EPS = 1e-6
NEG_INF = -1e30
QK_SCALE = 1.0 / math.sqrt(HEAD_DIM)
SB_DEAD_TAIL = 104.0

LANES = 128
HEADS_PER_TILE = LANES // HEAD_DIM
N_PAIRS = N_HEADS // HEADS_PER_TILE
Q_TILE = MOBA_BLOCK
PROJ_ROWS = 256
MERGE_ROWS = 512
STREAM_PAGES = 8
VMEM_LIMIT = 56 * 1024 * 1024

_NT = (((1,), (1,)), ((), ()))


def _split3(x):
    a = x.astype(BF16)
    r = x - a.astype(F32)
    b = r.astype(BF16)
    c = (r - b.astype(F32)).astype(BF16)
    return a, b, c


def _split2(x):
    a = x.astype(BF16)
    b = (x - a.astype(F32)).astype(BF16)
    return a, b


def _sigmoid(x):
    return 1.0 / (1.0 + jnp.exp(-x))


def _neg_log_sigmoid_neg(z):
    return jnp.maximum(z, 0.0) + jnp.log(1.0 + jnp.exp(-jnp.abs(z)))


def _topk_rank(gate, axis_index):
    rank = jnp.zeros(gate.shape, F32)
    for jp in range(gate.shape[0]):
        row = gate[jp:jp + 1]
        rank = rank + jnp.where(row > gate, 1.0,
                                jnp.where(row == gate, jnp.where(jp < axis_index, 1.0, 0.0), 0.0))
    return rank


def _proj_kernel(prompt, x_ref, ng_ref, w_ref, bg_ref, qg_ref, kg_ref, cos_ref, s1_ref, s2_ref, seg_ref,
                 *outs):
    x = x_ref[...]
    ms = jnp.mean(x * x, axis=-1, keepdims=True)
    h = (x * lax.rsqrt(ms + EPS) * ng_ref[...]).astype(BF16)

    def col(c):
        return jnp.dot(h, w_ref[:, c * BRANCH_W:(c + 1) * BRANCH_W], preferred_element_type=F32)

    def head_norm_rope(u, g):
        ss = jnp.dot((u * u).astype(BF16), seg_ref[...], preferred_element_type=F32)
        y = u * lax.rsqrt(ss * (1.0 / HEAD_DIM) + EPS) * g
        return (y * cos_ref[...] + pltpu.roll(y, BRANCH_W - ROT_DIM // 2, 1) * s1_ref[...]
                + pltpu.roll(y, ROT_DIM // 2, 1) * s2_ref[...])

    def silu(u):
        return u * _sigmoid(u)

    qa = head_norm_rope(col(0), qg_ref[...]) * QK_SCALE
    ka = head_norm_rope(col(1), kg_ref[...])
    va = col(2)
    sga = silu(col(3))
    qb = col(4) * QK_SCALE
    kb = col(5)
    vb = col(6)
    sgb = silu(col(7))
    gm = jnp.dot(h, w_ref[:, 8 * BRANCH_W:], preferred_element_type=F32) + bg_ref[...]
    g = _sigmoid(gm)

    if prompt:
        (qa_ref, katbf_ref, vabf_ref, qb_ref, kbtbf_ref, vbbf_ref, kat_ref, vat_ref, kbt_ref, vbt_ref,
         sga_ref, sgb_ref, g_ref, kmean_ref) = outs
        qa_ref[...] = qa.astype(BF16)
        qb_ref[...] = qb.astype(BF16)
        vabf_ref[...] = va.astype(BF16)
        vbbf_ref[...] = vb.astype(BF16)
        ka_t, kb_t = ka.T, kb.T
        katbf_ref[0] = ka_t.astype(BF16)
        kbtbf_ref[0] = kb_t.astype(BF16)
        kat_ref[0] = ka_t
        vat_ref[0] = va.T
        kbt_ref[0] = kb_t
        vbt_ref[0] = vb.T
        kmean_ref[0] = jnp.mean(ka, axis=0, keepdims=True)
    else:
        qa_ref, ka_ref, va_ref, qb_ref, kb_ref, vb_ref, sga_ref, sgb_ref, g_ref = outs
        qa_ref[...] = qa
        qb_ref[...] = qb
        ka_ref[...] = ka
        va_ref[...] = va
        kb_ref[...] = kb
        vb_ref[...] = vb
    sga_ref[...] = sga
    sgb_ref[...] = sgb
    g_ref[...] = g


def _proj(x, ng, w_bf, bg, qg, kg, cos, s1, s2, seg, *, prompt, seq=None):
    n = x.shape[0]
    tm = PROJ_ROWS if prompt else n
    nt = n // tm
    in_cols = w_bf.shape[1]
    row = lambda i: (i, 0)
    const = lambda i: (0, 0)
    wide = pl.BlockSpec((tm, BRANCH_W), row)
    tab_blocks = cos.shape[0] // tm
    table = pl.BlockSpec((tm, BRANCH_W), lambda i: (i % tab_blocks, 0))
    in_specs = [
        pl.BlockSpec((tm, D_MODEL), row),
        pl.BlockSpec((1, D_MODEL), const),
        pl.BlockSpec((D_MODEL, in_cols), const),
        pl.BlockSpec((1, 2 * D_MODEL), const),
        pl.BlockSpec((1, BRANCH_W), const),
        pl.BlockSpec((1, BRANCH_W), const),
        table, table, table,
        pl.BlockSpec((BRANCH_W, BRANCH_W), const),
    ]
    f32w = jax.ShapeDtypeStruct((n, BRANCH_W), F32)
    bf16w = jax.ShapeDtypeStruct((n, BRANCH_W), BF16)
    gate = jax.ShapeDtypeStruct((n, 2 * D_MODEL), F32)
    gate_spec = pl.BlockSpec((tm, 2 * D_MODEL), row)
    if prompt:
        tiles_per_seq = seq // tm
        ktbf = jax.ShapeDtypeStruct((nt, BRANCH_W, tm), BF16)
        ktbf_spec = pl.BlockSpec((1, BRANCH_W, tm), lambda i: (i, 0, 0))
        kvt = jax.ShapeDtypeStruct((n // seq, BRANCH_W, seq), F32)
        kvt_spec = pl.BlockSpec((1, BRANCH_W, tm), lambda i: (i // tiles_per_seq, 0, i % tiles_per_seq))
        kmean = jax.ShapeDtypeStruct((nt, 1, BRANCH_W), F32)
        kmean_spec = pl.BlockSpec((1, 1, BRANCH_W), lambda i: (i, 0, 0))
        out_shape = (bf16w, ktbf, bf16w, bf16w, ktbf, bf16w, kvt, kvt, kvt, kvt, f32w, f32w, gate, kmean)
        out_specs = (wide, ktbf_spec, wide, wide, ktbf_spec, wide, kvt_spec, kvt_spec, kvt_spec, kvt_spec,
                     wide, wide, gate_spec, kmean_spec)
    else:
        out_shape = (f32w,) * 8 + (gate,)
        out_specs = (wide,) * 8 + (gate_spec,)
    return pl.pallas_call(
        functools.partial(_proj_kernel, prompt),
        grid=(nt,),
        in_specs=in_specs,
        out_specs=out_specs,
        out_shape=out_shape,
        compiler_params=pltpu.CompilerParams(dimension_semantics=("arbitrary",), vmem_limit_bytes=VMEM_LIMIT),
        name="proj_prompt" if prompt else "proj_sample",
    )(x, ng, w_bf, bg, qg, kg, cos, s1, s2, seg)


def _head_masked(q, hh):
    lane = lax.broadcasted_iota(jnp.int32, (1, LANES), 1)
    keep = jnp.where((lane // HEAD_DIM) == hh, 1.0, 0.0)
    return (q.astype(F32) * keep).astype(BF16)


def _pick_head_lanes(per_head):
    lane = lax.broadcasted_iota(jnp.int32, (1, LANES), 1)
    out = per_head[-1]
    for hh in range(HEADS_PER_TILE - 2, -1, -1):
        out = jnp.where(lane < (hh + 1) * HEAD_DIM, per_head[hh], out)
    return out


def _attn_specs(b, t, nblk):
    in_specs = [
        pl.BlockSpec((1, Q_TILE, LANES), lambda bi, p, i: (bi, i, p)),
        pl.BlockSpec((1, nblk, LANES, Q_TILE), lambda bi, p, i: (bi, 0, p, 0)),
        pl.BlockSpec((1, t, LANES), lambda bi, p, i: (bi, 0, p)),
    ]
    out_spec = pl.BlockSpec((1, Q_TILE, LANES), lambda bi, p, i: (bi, i, p))
    return in_specs, out_spec


def _key_block(kt_ref, v_ref, j):
    return kt_ref[0, j], v_ref[0, pl.ds(pl.multiple_of(j * Q_TILE, Q_TILE), Q_TILE), :]


def _moba_kernel(q_ref, kt_ref, v_ref, km_ref, e_ref, o_ref, acc_ref):
    i = pl.program_id(2)
    nblk = km_ref.shape[1]
    q = q_ref[0]
    km_parts = _split3(km_ref[0])
    blk = lax.broadcasted_iota(jnp.int32, (nblk, Q_TILE), 0)
    past = blk < i
    r = lax.broadcasted_iota(jnp.int32, (Q_TILE, Q_TILE), 0)
    c = lax.broadcasted_iota(jnp.int32, (Q_TILE, Q_TILE), 1)
    causal = c <= r

    kt_own, v_own = _key_block(kt_ref, v_ref, i)
    qh, stats = [], []
    for hh in range(HEADS_PER_TILE):
        qm = _head_masked(q, hh)
        gate_t = sum(lax.dot_general(p, qm, _NT, preferred_element_type=F32) for p in km_parts)
        rank = _topk_rank(jnp.where(past, gate_t, NEG_INF), blk)
        pen_t = jnp.where(past, jnp.where(rank < MOBA_TOPK, 0.0, NEG_INF), NEG_INF)
        pen_t = jnp.concatenate([pen_t, jnp.zeros((LANES - nblk, Q_TILE), F32)], axis=0)
        qh.append(jnp.concatenate([qm, pen_t.T.astype(BF16)], axis=1))
        s = jnp.dot(qm, kt_own, preferred_element_type=F32)
        s = jnp.where(causal, s, NEG_INF)
        m = jnp.max(s, axis=-1, keepdims=True)
        p = jnp.exp(s - m)
        acc_ref[hh] = jnp.dot(p.astype(BF16), v_own, preferred_element_type=F32)
        stats += [m, jnp.sum(p, axis=-1, keepdims=True)]

    def body(j, carry):
        kt_j, v_j = _key_block(kt_ref, v_ref, j)
        kt_aug = jnp.concatenate([kt_j, e_ref[j]], axis=0)
        scores = [jnp.dot(qh[hh], kt_aug, preferred_element_type=F32) for hh in range(HEADS_PER_TILE)]
        out = []
        for hh in range(HEADS_PER_TILE):
            m, l = carry[2 * hh:2 * hh + 2]
            s = scores[hh]
            m_new = jnp.maximum(m, jnp.max(s, axis=-1, keepdims=True))
            alpha = jnp.exp(m - m_new)
            p = jnp.exp(s - m_new)
            acc_ref[hh] = alpha * acc_ref[hh] + jnp.dot(p.astype(BF16), v_j, preferred_element_type=F32)
            out += [m_new, alpha * l + jnp.sum(p, axis=-1, keepdims=True)]
        return tuple(out)

    stats = lax.fori_loop(0, i, body, tuple(stats))
    o_ref[0] = _pick_head_lanes([acc_ref[hh] / stats[2 * hh + 1] for hh in range(HEADS_PER_TILE)])


def _moba_prompt(q_bf, kt_bf, v_bf, kmean, block_onehot):
    b, t, _ = q_bf.shape
    nblk = t // Q_TILE
    in_specs, out_spec = _attn_specs(b, t, nblk)
    in_specs += [
        pl.BlockSpec((1, nblk, LANES), lambda bi, p, i: (bi, 0, p)),
        pl.BlockSpec((nblk, LANES, Q_TILE), lambda bi, p, i: (0, 0, 0)),
    ]
    return pl.pallas_call(
        _moba_kernel,
        grid=(b, N_PAIRS, nblk),
        in_specs=in_specs,
        out_specs=out_spec,
        out_shape=jax.ShapeDtypeStruct((b, t, BRANCH_W), F32),
        scratch_shapes=[pltpu.VMEM((HEADS_PER_TILE, Q_TILE, LANES), F32)],
        compiler_params=pltpu.CompilerParams(dimension_semantics=("arbitrary",) * 3, vmem_limit_bytes=VMEM_LIMIT),
        name="moba_prompt",
    )(q_bf, kt_bf, v_bf, kmean, block_onehot)


def _sb_kernel(q_ref, kt_ref, v_ref, u2_ref, o_ref, acc_ref):
    i = pl.program_id(2)
    q = q_ref[0]
    r = lax.broadcasted_iota(jnp.int32, (Q_TILE, Q_TILE), 0)
    c = lax.broadcasted_iota(jnp.int32, (Q_TILE, Q_TILE), 1)
    strict = c < r
    qh = [_head_masked(q, hh) for hh in range(HEADS_PER_TILE)]

    def block(j, tails, mask):
        kt_j, v_j = _key_block(kt_ref, v_ref, j)
        heads = range(HEADS_PER_TILE)
        zs = [jnp.dot(qh[hh], kt_j, preferred_element_type=F32) for hh in heads]
        nlms = [_neg_log_sigmoid_neg(z) for z in zs]
        if mask is not None:
            nlms = [jnp.where(mask, nlm, 0.0) for nlm in nlms]
        laters = [jnp.dot(jnp.concatenate(_split2(nlm), axis=1), u2_ref[...], preferred_element_type=F32)
                  for nlm in nlms]
        out = []
        for hh in heads:
            a = jnp.exp(zs[hh] - nlms[hh] - (tails[hh] + laters[hh]))
            if mask is not None:
                a = jnp.where(mask, a, 0.0)
            acc_ref[hh] += jnp.dot(a.astype(BF16), v_j, preferred_element_type=F32)
            out.append(tails[hh] + jnp.sum(nlms[hh], axis=-1, keepdims=True))
        return tuple(out)

    acc_ref[...] = jnp.zeros_like(acc_ref)
    tails = block(i, (jnp.zeros((Q_TILE, 1), F32),) * HEADS_PER_TILE, strict)

    def live(carry):
        j, tails = carry[0], carry[1:]
        lowest = functools.reduce(jnp.minimum, tails)
        return jnp.logical_and(j >= 0, jnp.min(lowest) < SB_DEAD_TAIL)

    def step(carry):
        j, tails = carry[0], carry[1:]
        return (j - 1,) + block(j, tails, None)

    lax.while_loop(live, step, (i - 1,) + tails)
    o_ref[0] = _pick_head_lanes([acc_ref[hh] for hh in range(HEADS_PER_TILE)])


def _sb_prompt(q_bf, kt_bf, v_bf, upper2):
    b, t, _ = q_bf.shape
    nblk = t // Q_TILE
    in_specs, out_spec = _attn_specs(b, t, nblk)
    in_specs.append(pl.BlockSpec((2 * Q_TILE, Q_TILE), lambda bi, p, i: (0, 0)))
    return pl.pallas_call(
        _sb_kernel,
        grid=(b, N_PAIRS, nblk),
        in_specs=in_specs,
        out_specs=out_spec,
        out_shape=jax.ShapeDtypeStruct((b, t, BRANCH_W), F32),
        scratch_shapes=[pltpu.VMEM((HEADS_PER_TILE, Q_TILE, LANES), F32)],
        compiler_params=pltpu.CompilerParams(dimension_semantics=("arbitrary",) * 3, vmem_limit_bytes=VMEM_LIMIT),
        name="sb_prompt",
    )(q_bf, kt_bf, v_bf, upper2)


def _merge_kernel(x_ref, oa_ref, ob_ref, sga_ref, sgb_ref, g_ref, wa_ref, wb_ref, wo_ref, y_ref):
    a = jnp.dot((oa_ref[...] * sga_ref[...]).astype(BF16), wa_ref[...], preferred_element_type=F32)
    b = jnp.dot((ob_ref[...] * sgb_ref[...]).astype(BF16), wb_ref[...], preferred_element_type=F32)
    g = g_ref[...]
    merged = g[:, :D_MODEL] * a + g[:, D_MODEL:] * b
    y_ref[...] = x_ref[...] + jnp.dot(merged.astype(BF16), wo_ref[...], preferred_element_type=F32)


def _merge(x, oa, ob, sga, sgb, g, wa_bf, wb_bf, wo_bf, *, name):
    n = x.shape[0]
    tm = min(MERGE_ROWS, n)
    row = lambda i: (i, 0)
    const = lambda i: (0, 0)
    wide = pl.BlockSpec((tm, BRANCH_W), row)
    return pl.pallas_call(
        _merge_kernel,
        grid=(n // tm,),
        in_specs=[
            pl.BlockSpec((tm, D_MODEL), row), wide, wide, wide, wide,
            pl.BlockSpec((tm, 2 * D_MODEL), row),
            pl.BlockSpec((BRANCH_W, D_MODEL), const),
            pl.BlockSpec((BRANCH_W, D_MODEL), const),
            pl.BlockSpec((D_MODEL, D_MODEL), const),
        ],
        out_specs=pl.BlockSpec((tm, D_MODEL), row),
        out_shape=jax.ShapeDtypeStruct((n, D_MODEL), F32),
        compiler_params=pltpu.CompilerParams(dimension_semantics=("arbitrary",), vmem_limit_bytes=VMEM_LIMIT),
        name=name,
    )(x, oa, ob, sga, sgb, g, wa_bf, wb_bf, wo_bf)


def _stream_kernel(layer, nb, n_pages, pt_ref, qm_ref, qs_ref, mk_hbm, sk_hbm, sv_hbm, w_ref,
                   ob_ref, s_ref, sel_ref, mk_buf, sk_buf, sv_buf, sems, tail_ref, acc_ref):
    b = pl.program_id(0)
    c = pl.program_id(1)
    n_chunks = n_pages // STREAM_PAGES
    total = nb * n_chunks
    step = b * n_chunks + c
    slot = step % 2
    page_len = sk_buf.shape[-1]
    pages_per_block = MOBA_BLOCK // page_len
    n_blocks = n_pages // pages_per_block

    def copies(bb, cc, slot_):
        out = []
        for g in range(STREAM_PAGES):
            page = pt_ref[bb, n_pages - 1 - (cc * STREAM_PAGES + g)]
            for ci, (hbm, buf) in enumerate(((mk_hbm, mk_buf), (sk_hbm, sk_buf), (sv_hbm, sv_buf))):
                out.append(pltpu.make_async_copy(hbm.at[layer, page], buf.at[slot_, g], sems.at[slot_, ci]))
        return out

    @pl.when(step == 0)
    def _():
        for cp in copies(0, 0, 0):
            cp.start()

    @pl.when(step + 1 < total)
    def _():
        nxt = step + 1
        for cp in copies(nxt // n_chunks, nxt % n_chunks, 1 - slot):
            cp.start()

    for cp in copies(b, c, slot):
        cp.wait()

    @pl.when(c == 0)
    def _():
        tail_ref[...] = jnp.zeros_like(tail_ref)
        acc_ref[...] = jnp.zeros_like(acc_ref)

    pages = range(STREAM_PAGES)
    for g in pages:
        s_ref[0, n_pages - 1 - (c * STREAM_PAGES + g)] = jnp.sum(mk_buf[slot, g] * qm_ref[0], axis=1)

    z = jnp.stack([jnp.sum(sk_buf[slot, g] * qs_ref[0], axis=1) for g in pages])
    nlm = _neg_log_sigmoid_neg(z)
    hi, lo = _split2(nlm.reshape(STREAM_PAGES * N_HEADS, page_len))
    later = (jnp.dot(hi, w_ref[...], preferred_element_type=F32)
             + jnp.dot(lo, w_ref[...], preferred_element_type=F32)).reshape(z.shape)
    page_sum = jnp.sum(nlm, axis=-1, keepdims=True)
    tails = [tail_ref[...]]
    for g in pages:
        tails.append(tails[-1] + page_sum[g])
    a = jnp.exp(z - nlm - (jnp.stack(tails[:-1]) + later))
    tail_ref[...] = tails[-1]
    acc = acc_ref[...]
    for g in pages:
        acc = acc + a[g][:, None, :] * sv_buf[slot, g]
    acc_ref[...] = acc

    @pl.when(c == n_chunks - 1)
    def _():
        ob_ref[0] = jnp.sum(acc_ref[...], axis=-1)
        scores = s_ref[0].reshape(n_blocks, pages_per_block, N_HEADS, page_len)
        gate = jnp.sum(jnp.sum(scores, axis=1), axis=-1, keepdims=True) * (1.0 / MOBA_BLOCK)
        gate = jnp.broadcast_to(gate, (n_blocks, N_HEADS, page_len))
        blk = lax.broadcasted_iota(jnp.int32, gate.shape, 0)
        rank = _topk_rank(gate, blk)
        for t in range(MOBA_TOPK):
            sel_ref[0, t] = jnp.sum(jnp.where(rank == float(t), blk.astype(F32), 0.0), axis=0).astype(jnp.int32)


def _decode_stream(layer, page_table, q_moba, q_sb, mk, sk, sv, lower):
    nb, n_pages = page_table.shape
    page_len = mk.shape[-1]
    n_chunks = n_pages // STREAM_PAGES
    any_spec = pl.BlockSpec(memory_space=pl.ANY)
    buf = pltpu.VMEM((2, STREAM_PAGES, N_HEADS, HEAD_DIM, page_len), F32)
    q_spec = pl.BlockSpec((1, N_HEADS, HEAD_DIM, page_len), lambda b, c, pt: (b, 0, 0, 0))
    grid_spec = pltpu.PrefetchScalarGridSpec(
        num_scalar_prefetch=1,
        grid=(nb, n_chunks),
        in_specs=[q_spec, q_spec, any_spec, any_spec, any_spec,
                  pl.BlockSpec((page_len, page_len), lambda b, c, pt: (0, 0))],
        out_specs=(
            pl.BlockSpec((1, N_HEADS, HEAD_DIM), lambda b, c, pt: (b, 0, 0)),
            pl.BlockSpec((1, n_pages, N_HEADS, page_len), lambda b, c, pt: (b, 0, 0, 0)),
            pl.BlockSpec((1, MOBA_TOPK, N_HEADS, page_len), lambda b, c, pt: (b, 0, 0, 0)),
        ),
        scratch_shapes=[buf, buf, buf, pltpu.SemaphoreType.DMA((2, 3)),
                        pltpu.VMEM((N_HEADS, 1), F32), pltpu.VMEM((N_HEADS, HEAD_DIM, page_len), F32)],
    )
    return pl.pallas_call(
        functools.partial(_stream_kernel, layer, nb, n_pages),
        grid_spec=grid_spec,
        out_shape=(jax.ShapeDtypeStruct((nb, N_HEADS, HEAD_DIM), F32),
                   jax.ShapeDtypeStruct((nb, n_pages, N_HEADS, page_len), F32),
                   jax.ShapeDtypeStruct((nb, MOBA_TOPK, N_HEADS, page_len), jnp.int32)),
        compiler_params=pltpu.CompilerParams(dimension_semantics=("arbitrary", "arbitrary"),
                                             vmem_limit_bytes=VMEM_LIMIT),
        name="decode_stream",
    )(page_table, q_moba, q_sb, mk, sk, sv, lower)


def _moba_decode_kernel(layer, nb, pt_ref, selsm_ref, s_ref, selv_ref, q_ref, kn_ref, vn_ref, mv_hbm, o_ref,
                        vbuf, sems, p_ref):
    b = pl.program_id(0)
    slot = b % 2
    n_pages, _, page_len = s_ref.shape[1:]
    pages_per_block = MOBA_BLOCK // page_len

    def tiles():
        return [(h, t, half) for h in range(N_HEADS) for t in range(MOBA_TOPK) for half in range(pages_per_block)]

    def copies(bb, slot_):
        out = []
        for h, t, half in tiles():
            page = pt_ref[bb, selsm_ref[bb, t * N_HEADS + h] * pages_per_block + half]
            out.append(pltpu.make_async_copy(mv_hbm.at[layer, page, h], vbuf.at[slot_, h, t, half], sems.at[slot_]))
        return out

    @pl.when(b == 0)
    def _():
        for cp in copies(0, 0):
            cp.start()

    @pl.when(b + 1 < nb)
    def _():
        for cp in copies(b + 1, 1 - slot):
            cp.start()

    s = s_ref[0]
    page_blk = lax.broadcasted_iota(jnp.int32, s.shape, 0) // pages_per_block
    sm = jnp.full(s.shape, NEG_INF, F32)
    for t in range(MOBA_TOPK):
        sm = jnp.where(page_blk == selv_ref[0, t][None], s, sm)
    s_new = jnp.sum(q_ref[0] * kn_ref[0], axis=-1, keepdims=True)
    m = jnp.maximum(jnp.max(jnp.max(sm, axis=0), axis=-1, keepdims=True), s_new)
    p = jnp.exp(sm - m)
    p_new = jnp.exp(s_new - m)
    l = jnp.sum(jnp.sum(p, axis=0), axis=-1, keepdims=True) + p_new
    p_ref[...] = p

    for cp in copies(b, slot):
        cp.wait()

    head = lax.broadcasted_iota(jnp.int32, (N_HEADS, HEAD_DIM), 0)
    acc = p_new * vn_ref[0]
    for h, t, half in tiles():
        page = selsm_ref[b, t * N_HEADS + h] * pages_per_block + half
        res = lax.dot_general(p_ref[page].astype(BF16), vbuf[slot, h, t, half].astype(BF16), _NT,
                              preferred_element_type=F32)
        acc = acc + jnp.where(head == h, res, 0.0)
    o_ref[0] = acc / l


def _moba_decode(layer, page_table, sel_scalar, scores, sel_vec, q, k_new, v_new, mv):
    nb, n_pages = page_table.shape
    page_len = mv.shape[-1]
    pages_per_block = MOBA_BLOCK // page_len
    any_spec = pl.BlockSpec(memory_space=pl.ANY)
    row = pl.BlockSpec((1, N_HEADS, HEAD_DIM), lambda b, pt, sl: (b, 0, 0))
    grid_spec = pltpu.PrefetchScalarGridSpec(
        num_scalar_prefetch=2,
        grid=(nb,),
        in_specs=[
            pl.BlockSpec((1, n_pages, N_HEADS, page_len), lambda b, pt, sl: (b, 0, 0, 0)),
            pl.BlockSpec((1, MOBA_TOPK, N_HEADS, page_len), lambda b, pt, sl: (b, 0, 0, 0)),
            row, row, row, any_spec,
        ],
        out_specs=row,
        scratch_shapes=[pltpu.VMEM((2, N_HEADS, MOBA_TOPK, pages_per_block, HEAD_DIM, page_len), F32),
                        pltpu.SemaphoreType.DMA((2,)),
                        pltpu.VMEM((n_pages, N_HEADS, page_len), F32)],
    )
    return pl.pallas_call(
        functools.partial(_moba_decode_kernel, layer, nb),
        grid_spec=grid_spec,
        out_shape=jax.ShapeDtypeStruct((nb, N_HEADS, HEAD_DIM), F32),
        compiler_params=pltpu.CompilerParams(dimension_semantics=("arbitrary",), vmem_limit_bytes=VMEM_LIMIT),
        name="moba_decode",
    )(page_table, sel_scalar, scores, sel_vec, q, k_new, v_new, mv)


def _rope_tables(pos):
    half = ROT_DIM // 2
    inv = ROPE_THETA ** (-jnp.arange(half, dtype=F32) * 2.0 / ROT_DIM)
    ang = pos[:, None] * inv[None, :]
    cos, sin = jnp.cos(ang), jnp.sin(ang)
    n = pos.shape[0]
    pad = jnp.zeros((n, HEAD_DIM - ROT_DIM), F32)
    zero = jnp.zeros((n, half), F32)
    cos_h = jnp.concatenate([cos, cos, pad + 1.0], axis=1)
    s1_h = jnp.concatenate([-sin, zero, pad], axis=1)
    s2_h = jnp.concatenate([zero, sin, pad], axis=1)
    return tuple(jnp.tile(t, (1, N_HEADS)) for t in (cos_h, s1_h, s2_h))


def kernel(x_prompt, x_sample, cache_moba_k, cache_moba_v, cache_sb_k, cache_sb_v, page_table,
           norm_g, w_in, b_gate, q_norm_g, k_norm_g, w_branch_a, w_branch_b, w_out):
    depth = w_in.shape[0]
    bsz, seq, _ = x_prompt.shape
    nb, dec_seq, _ = x_sample.shape
    page_len = cache_moba_k.shape[2]
    n_pages = page_table.shape[1]
    past_len = n_pages * page_len
    n_prompt = bsz * seq
    n_qblk = seq // Q_TILE
    assert dec_seq == 1 and seq % Q_TILE == 0 and MOBA_BLOCK % page_len == 0 and n_qblk <= LANES
    assert n_pages % STREAM_PAGES == 0 and past_len % MOBA_BLOCK == 0 and page_len == LANES

    lane_head = jnp.arange(BRANCH_W) // HEAD_DIM
    seg = (lane_head[:, None] == lane_head[None, :]).astype(BF16)
    idx = jnp.arange(Q_TILE)
    later_q = (idx[:, None] > idx[None, :]).astype(BF16)
    later_q2 = jnp.concatenate([later_q, later_q], axis=0)
    later_p = later_q[:page_len, :page_len]
    block_onehot = jnp.broadcast_to(
        (jnp.arange(LANES)[None, :, None] == jnp.arange(n_qblk)[:, None, None]), (n_qblk, LANES, Q_TILE)).astype(BF16)

    cos_p, s1_p, s2_p = _rope_tables(jnp.arange(seq, dtype=F32))
    cos_s, s1_s, s2_s = _rope_tables(jnp.full((nb,), past_len, F32))

    mk_t, mv_t, sk_t, sv_t = (jnp.transpose(c, (0, 1, 3, 4, 2))
                              for c in (cache_moba_k, cache_moba_v, cache_sb_k, cache_sb_v))

    def lane_replicated(q):
        return jnp.broadcast_to(q.reshape(nb, N_HEADS, HEAD_DIM, 1), (nb, N_HEADS, HEAD_DIM, page_len))

    yp = x_prompt.reshape(n_prompt, D_MODEL)
    ys = x_sample.reshape(nb, D_MODEL)
    rows_p = [[] for _ in range(4)]
    rows_s = [[] for _ in range(4)]
    for l in range(depth):
        w_bf = w_in[l].astype(BF16)
        wa_bf, wb_bf, wo_bf = (w[l].astype(BF16) for w in (w_branch_a, w_branch_b, w_out))
        params = (norm_g[l][None, :], w_bf, b_gate[l][None, :],
                  jnp.tile(q_norm_g[l], N_HEADS)[None, :], jnp.tile(k_norm_g[l], N_HEADS)[None, :])

        (qa, kat_bf, va_bf, qb, kbt_bf, vb_bf, ka_t, va_t, kb_t, vb_t, sga, sgb, g, kmean) = _proj(
            yp, *params, cos_p, s1_p, s2_p, seg, prompt=True, seq=seq)
        as3 = lambda a: a.reshape(bsz, seq, BRANCH_W)
        kt4 = lambda a: a.reshape(bsz, n_qblk, BRANCH_W, Q_TILE)
        oa = _moba_prompt(as3(qa), kt4(kat_bf), as3(va_bf), kmean.reshape(bsz, n_qblk, BRANCH_W), block_onehot)
        ob = _sb_prompt(as3(qb), kt4(kbt_bf), as3(vb_bf), later_q2)
        yp = _merge(yp, oa.reshape(n_prompt, BRANCH_W), ob.reshape(n_prompt, BRANCH_W), sga, sgb, g,
                    wa_bf, wb_bf, wo_bf, name="merge_prompt")
        for dst, src in zip(rows_p, (ka_t, va_t, kb_t, vb_t)):
            dst.append(jnp.transpose(src.reshape(bsz, N_HEADS, HEAD_DIM, seq), (0, 3, 1, 2)))

        (qa_s, ka_s, va_s, qb_s, kb_s, vb_s, sga_s, sgb_s, g_s) = _proj(
            ys, *params, cos_s, s1_s, s2_s, seg, prompt=False)
        heads = lambda a: a.reshape(nb, N_HEADS, HEAD_DIM)
        ob_s, scores, sel_vec = _decode_stream(l, page_table, lane_replicated(qa_s), lane_replicated(qb_s),
                                               mk_t, sk_t, sv_t, later_p)
        sel_scalar = sel_vec[:, :, :, 0].reshape(nb, MOBA_TOPK * N_HEADS)
        oa_s = _moba_decode(l, page_table, sel_scalar, scores, sel_vec, heads(qa_s), heads(ka_s), heads(va_s), mv_t)
        ys = _merge(ys, oa_s.reshape(nb, BRANCH_W), ob_s.reshape(nb, BRANCH_W), sga_s, sgb_s, g_s,
                    wa_bf, wb_bf, wo_bf, name="merge_sample")
        for dst, src in zip(rows_s, (ka_s, va_s, kb_s, vb_s)):
            dst.append(src.reshape(nb, 1, N_HEADS, HEAD_DIM))

    return (yp.reshape(bsz, seq, D_MODEL), ys.reshape(nb, 1, D_MODEL),
            *(jnp.stack(r) for r in rows_p), *(jnp.stack(r) for r in rows_s))
```

```python
import functools
import math

import jax
import jax.numpy as jnp
from jax import lax
from jax.experimental import pallas as pl
from jax.experimental.pallas import tpu as pltpu

F32 = jnp.float32
BF16 = jnp.bfloat16

D_MODEL = 1024
HEAD_DIM = 64
N_HEADS = 8
BRANCH_W = N_HEADS * HEAD_DIM
ROT_DIM = HEAD_DIM // 4
ROPE_THETA = 500000.0
MOBA_BLOCK = 256
MOBA_TOPK = 3
EPS = 1e-6
NEG_INF = -1e30
QK_SCALE = 1.0 / math.sqrt(HEAD_DIM)
SB_DEAD_TAIL = 104.0

LANES = 128
HEADS_PER_TILE = LANES // HEAD_DIM
N_PAIRS = N_HEADS // HEADS_PER_TILE
Q_TILE = MOBA_BLOCK
PROJ_ROWS = 256
MERGE_ROWS = 512
STREAM_PAGES = 16
VMEM_LIMIT = 56 * 1024 * 1024

_NT = (((1,), (1,)), ((), ()))


def _split3(x):
    a = x.astype(BF16)
    r = x - a.astype(F32)
    b = r.astype(BF16)
    c = (r - b.astype(F32)).astype(BF16)
    return a, b, c


def _split2(x):
    a = x.astype(BF16)
    b = (x - a.astype(F32)).astype(BF16)
    return a, b


def _sigmoid(x):
    return 1.0 / (1.0 + jnp.exp(-x))


def _neg_log_sigmoid_neg(z):
    return jnp.maximum(z, 0.0) + jnp.log(1.0 + jnp.exp(-jnp.abs(z)))


def _topk_rank(gate, axis_index):
    rank = jnp.zeros(gate.shape, F32)
    for jp in range(gate.shape[0]):
        row = gate[jp:jp + 1]
        rank = rank + jnp.where(row > gate, 1.0,
                                jnp.where(row == gate, jnp.where(jp < axis_index, 1.0, 0.0), 0.0))
    return rank


def _proj_kernel(prompt, n_carried, x_ref, ng_ref, w_ref, bg_ref, qg_ref, kg_ref, cos_ref, s1_ref, s2_ref,
                 seg_ref, *rest):
    outs = rest[n_carried:]
    x = x_ref[...]
    ms = jnp.mean(x * x, axis=-1, keepdims=True)
    h = (x * lax.rsqrt(ms + EPS) * ng_ref[...]).astype(BF16)

    def col(c):
        return jnp.dot(h, w_ref[:, c * BRANCH_W:(c + 1) * BRANCH_W], preferred_element_type=F32)

    def head_norm_rope(u, g):
        ss = jnp.dot((u * u).astype(BF16), seg_ref[...], preferred_element_type=F32)
        y = u * lax.rsqrt(ss * (1.0 / HEAD_DIM) + EPS) * g
        return (y * cos_ref[...] + pltpu.roll(y, BRANCH_W - ROT_DIM // 2, 1) * s1_ref[...]
                + pltpu.roll(y, ROT_DIM // 2, 1) * s2_ref[...])

    def silu(u):
        return u * _sigmoid(u)

    qa = head_norm_rope(col(0), qg_ref[...]) * QK_SCALE
    ka = head_norm_rope(col(1), kg_ref[...])
    va = col(2)
    sga = silu(col(3))
    qb = col(4) * QK_SCALE
    kb = col(5)
    vb = col(6)
    sgb = silu(col(7))
    gm = jnp.dot(h, w_ref[:, 8 * BRANCH_W:], preferred_element_type=F32) + bg_ref[...]
    g = _sigmoid(gm)

    if prompt:
        (qa_ref, katbf_ref, vabf_ref, qb_ref, kbtbf_ref, vbbf_ref, kat_ref, vat_ref, kbt_ref, vbt_ref,
         sga_ref, sgb_ref, g_ref, kmean_ref) = outs
        qa_ref[...] = qa.astype(BF16)
        qb_ref[...] = qb.astype(BF16)
        vabf_ref[...] = va.astype(BF16)
        vbbf_ref[...] = vb.astype(BF16)
        ka_t, kb_t = ka.T, kb.T
        katbf_ref[0] = ka_t.astype(BF16)
        kbtbf_ref[0] = kb_t.astype(BF16)
        for ref, val in ((kat_ref, ka_t), (vat_ref, va.T), (kbt_ref, kb_t), (vbt_ref, vb.T)):
            for d in range(ref.shape[0]):
                ref[d, 0] = val
        kmean_ref[0] = jnp.mean(ka, axis=0, keepdims=True)
    else:
        qa_ref, ka_ref, va_ref, qb_ref, kb_ref, vb_ref, sga_ref, sgb_ref, g_ref = outs
        qa_ref[...] = qa
        qb_ref[...] = qb
        ka_ref[...] = ka
        va_ref[...] = va
        kb_ref[...] = kb
        vb_ref[...] = vb
    sga_ref[...] = sga
    sgb_ref[...] = sgb
    g_ref[...] = g


def _proj(x, ng, w_bf, bg, qg, kg, cos, s1, s2, seg, *, prompt, seq=None, layer=0, depth=1, carried=()):
    n = x.shape[0]
    tm = PROJ_ROWS if prompt else n
    nt = n // tm
    in_cols = w_bf.shape[1]
    row = lambda i: (i, 0)
    const = lambda i: (0, 0)
    wide = pl.BlockSpec((tm, BRANCH_W), row)
    tab_blocks = cos.shape[0] // tm
    table = pl.BlockSpec((tm, BRANCH_W), lambda i: (i % tab_blocks, 0))
    in_specs = [
        pl.BlockSpec((tm, D_MODEL), row),
        pl.BlockSpec((1, D_MODEL), const),
        pl.BlockSpec((D_MODEL, in_cols), const),
        pl.BlockSpec((1, 2 * D_MODEL), const),
        pl.BlockSpec((1, BRANCH_W), const),
        pl.BlockSpec((1, BRANCH_W), const),
        table, table, table,
        pl.BlockSpec((BRANCH_W, BRANCH_W), const),
    ]
    f32w = jax.ShapeDtypeStruct((n, BRANCH_W), F32)
    bf16w = jax.ShapeDtypeStruct((n, BRANCH_W), BF16)
    gate = jax.ShapeDtypeStruct((n, 2 * D_MODEL), F32)
    gate_spec = pl.BlockSpec((tm, 2 * D_MODEL), row)
    if prompt:
        tiles_per_seq = seq // tm
        ktbf = jax.ShapeDtypeStruct((nt, BRANCH_W, tm), BF16)
        ktbf_spec = pl.BlockSpec((1, BRANCH_W, tm), lambda i: (i, 0, 0))
        kvt = jax.ShapeDtypeStruct((depth, n // seq, BRANCH_W, seq), F32)
        slabs = depth if layer == 0 else 1
        kvt_spec = pl.BlockSpec((slabs, 1, BRANCH_W, tm),
                                lambda i: (layer, i // tiles_per_seq, 0, i % tiles_per_seq))
        kmean = jax.ShapeDtypeStruct((nt, 1, BRANCH_W), F32)
        kmean_spec = pl.BlockSpec((1, 1, BRANCH_W), lambda i: (i, 0, 0))
        out_shape = (bf16w, ktbf, bf16w, bf16w, ktbf, bf16w, kvt, kvt, kvt, kvt, f32w, f32w, gate, kmean)
        out_specs = (wide, ktbf_spec, wide, wide, ktbf_spec, wide, kvt_spec, kvt_spec, kvt_spec, kvt_spec,
                     wide, wide, gate_spec, kmean_spec)
    else:
        out_shape = (f32w,) * 8 + (gate,)
        out_specs = (wide,) * 8 + (gate_spec,)
    first_kv_out = 6
    aliases = {len(in_specs) + k: first_kv_out + k for k in range(len(carried))}
    in_specs += [pl.BlockSpec(memory_space=pl.ANY)] * len(carried)
    return pl.pallas_call(
        functools.partial(_proj_kernel, prompt, len(carried)),
        grid=(nt,),
        in_specs=in_specs,
        out_specs=out_specs,
        out_shape=out_shape,
        input_output_aliases=aliases,
        compiler_params=pltpu.CompilerParams(dimension_semantics=("arbitrary",), vmem_limit_bytes=VMEM_LIMIT),
        name="proj_prompt" if prompt else "proj_sample",
    )(x, ng, w_bf, bg, qg, kg, cos, s1, s2, seg, *carried)


def _head_masked(q, hh):
    lane = lax.broadcasted_iota(jnp.int32, (1, LANES), 1)
    keep = jnp.where((lane // HEAD_DIM) == hh, 1.0, 0.0)
    return (q.astype(F32) * keep).astype(BF16)


def _pick_head_lanes(per_head):
    lane = lax.broadcasted_iota(jnp.int32, (1, LANES), 1)
    out = per_head[-1]
    for hh in range(HEADS_PER_TILE - 2, -1, -1):
        out = jnp.where(lane < (hh + 1) * HEAD_DIM, per_head[hh], out)
    return out


def _attn_specs(b, t, nblk):
    in_specs = [
        pl.BlockSpec((1, Q_TILE, LANES), lambda bi, p, i: (bi, i, p)),
        pl.BlockSpec((1, nblk, LANES, Q_TILE), lambda bi, p, i: (bi, 0, p, 0)),
        pl.BlockSpec((1, t, LANES), lambda bi, p, i: (bi, 0, p)),
    ]
    out_spec = pl.BlockSpec((1, Q_TILE, LANES), lambda bi, p, i: (bi, i, p))
    return in_specs, out_spec


def _key_block(kt_ref, v_ref, j):
    return kt_ref[0, j], v_ref[0, pl.ds(pl.multiple_of(j * Q_TILE, Q_TILE), Q_TILE), :]


def _moba_kernel(q_ref, kt_ref, v_ref, km_ref, e_ref, o_ref, acc_ref):
    i = pl.program_id(2)
    nblk = km_ref.shape[1]
    q = q_ref[0]
    km_parts = _split3(km_ref[0])
    blk = lax.broadcasted_iota(jnp.int32, (nblk, Q_TILE), 0)
    past = blk < i
    r = lax.broadcasted_iota(jnp.int32, (Q_TILE, Q_TILE), 0)
    c = lax.broadcasted_iota(jnp.int32, (Q_TILE, Q_TILE), 1)
    causal = c <= r

    heads = range(HEADS_PER_TILE)
    lane = lax.broadcasted_iota(jnp.int32, (1, LANES), 1)
    in_head = [(lane // HEAD_DIM) == hh for hh in heads]

    def v_with_ones(v, hh):
        return jnp.where(jnp.broadcast_to(in_head[hh], v.shape), v, jnp.ones_like(v))

    kt_own, v_own = _key_block(kt_ref, v_ref, i)
    qh, ms = [], []
    for hh in heads:
        qm = _head_masked(q, hh)
        gate_t = sum(lax.dot_general(p, qm, _NT, preferred_element_type=F32) for p in km_parts)
        rank = _topk_rank(jnp.where(past, gate_t, NEG_INF), blk)
        pen_t = jnp.where(past, jnp.where(rank < MOBA_TOPK, 0.0, NEG_INF), NEG_INF)
        pen_t = jnp.concatenate([pen_t, jnp.zeros((LANES - nblk, Q_TILE), F32)], axis=0)
        qh.append(jnp.concatenate([qm, pen_t.T.astype(BF16)], axis=1))
        s = jnp.dot(qm, kt_own, preferred_element_type=F32)
        s = jnp.where(causal, s, NEG_INF)
        m = jnp.max(s, axis=-1, keepdims=True)
        acc_ref[hh] = jnp.dot(jnp.exp(s - m).astype(BF16), v_with_ones(v_own, hh), preferred_element_type=F32)
        ms.append(m)

    def attend(j, width, ms):
        kts = [jnp.concatenate([kt_ref[0, j + w], e_ref[j + w]], axis=0) for w in range(width)]
        v = v_ref[0, pl.ds(pl.multiple_of(j * Q_TILE, Q_TILE), width * Q_TILE), :]
        scores = [[jnp.dot(qh[hh], kt, preferred_element_type=F32) for kt in kts] for hh in heads]
        out = []
        for hh in heads:
            s = scores[hh][0] if width == 1 else jnp.concatenate(scores[hh], axis=1)
            m_new = jnp.maximum(ms[hh], jnp.max(s, axis=-1, keepdims=True))
            p = jnp.exp(s - m_new)
            acc_ref[hh] = (jnp.exp(ms[hh] - m_new) * acc_ref[hh]
                           + jnp.dot(p.astype(BF16), v_with_ones(v, hh), preferred_element_type=F32))
            out.append(m_new)
        return tuple(out)

    ms = lax.fori_loop(0, i // 2, lambda it, ms: attend(2 * it, 2, ms), tuple(ms))

    @pl.when(i % 2 == 1)
    def _():
        attend(i - 1, 1, ms)

    outs = []
    for hh in heads:
        acc = acc_ref[hh]
        row_sum = jnp.max(jnp.where(in_head[hh], 0.0, acc), axis=-1, keepdims=True)
        outs.append(acc / row_sum)
    o_ref[0] = _pick_head_lanes(outs)


def _moba_prompt(q_bf, kt_bf, v_bf, kmean, block_onehot):
    b, t, _ = q_bf.shape
    nblk = t // Q_TILE
    in_specs, out_spec = _attn_specs(b, t, nblk)
    in_specs += [
        pl.BlockSpec((1, nblk, LANES), lambda bi, p, i: (bi, 0, p)),
        pl.BlockSpec((nblk, LANES, Q_TILE), lambda bi, p, i: (0, 0, 0)),
    ]
    return pl.pallas_call(
        _moba_kernel,
        grid=(b, N_PAIRS, nblk),
        in_specs=in_specs,
        out_specs=out_spec,
        out_shape=jax.ShapeDtypeStruct((b, t, BRANCH_W), F32),
        scratch_shapes=[pltpu.VMEM((HEADS_PER_TILE, Q_TILE, LANES), F32)],
        compiler_params=pltpu.CompilerParams(dimension_semantics=("arbitrary",) * 3, vmem_limit_bytes=VMEM_LIMIT),
        name="moba_prompt",
    )(q_bf, kt_bf, v_bf, kmean, block_onehot)


def _sb_kernel(q_ref, kt_ref, v_ref, u2_ref, o_ref, acc_ref):
    i = pl.program_id(2)
    q = q_ref[0]
    r = lax.broadcasted_iota(jnp.int32, (Q_TILE, Q_TILE), 0)
    c = lax.broadcasted_iota(jnp.int32, (Q_TILE, Q_TILE), 1)
    strict = c < r
    qh = [_head_masked(q, hh) for hh in range(HEADS_PER_TILE)]

    def block(j, tails, mask):
        kt_j, v_j = _key_block(kt_ref, v_ref, j)
        heads = range(HEADS_PER_TILE)
        zs = [jnp.dot(qh[hh], kt_j, preferred_element_type=F32) for hh in heads]
        nlms = [_neg_log_sigmoid_neg(z) for z in zs]
        if mask is not None:
            nlms = [jnp.where(mask, nlm, 0.0) for nlm in nlms]
        laters = [jnp.dot(jnp.concatenate(_split2(nlm), axis=1), u2_ref[...], preferred_element_type=F32)
                  for nlm in nlms]
        out = []
        for hh in heads:
            a = jnp.exp(zs[hh] - nlms[hh] - (tails[hh] + laters[hh]))
            if mask is not None:
                a = jnp.where(mask, a, 0.0)
            acc_ref[hh] += jnp.dot(a.astype(BF16), v_j, preferred_element_type=F32)
            out.append(tails[hh] + jnp.sum(nlms[hh], axis=-1, keepdims=True))
        return tuple(out)

    acc_ref[...] = jnp.zeros_like(acc_ref)
    tails = block(i, (jnp.zeros((Q_TILE, 1), F32),) * HEADS_PER_TILE, strict)

    def live(carry):
        j, tails = carry[0], carry[1:]
        lowest = functools.reduce(jnp.minimum, tails)
        return jnp.logical_and(j >= 0, jnp.min(lowest) < SB_DEAD_TAIL)

    def step(carry):
        j, tails = carry[0], carry[1:]
        return (j - 1,) + block(j, tails, None)

    lax.while_loop(live, step, (i - 1,) + tails)
    o_ref[0] = _pick_head_lanes([acc_ref[hh] for hh in range(HEADS_PER_TILE)])


def _sb_prompt(q_bf, kt_bf, v_bf, upper2):
    b, t, _ = q_bf.shape
    nblk = t // Q_TILE
    in_specs, out_spec = _attn_specs(b, t, nblk)
    in_specs.append(pl.BlockSpec((2 * Q_TILE, Q_TILE), lambda bi, p, i: (0, 0)))
    return pl.pallas_call(
        _sb_kernel,
        grid=(b, N_PAIRS, nblk),
        in_specs=in_specs,
        out_specs=out_spec,
        out_shape=jax.ShapeDtypeStruct((b, t, BRANCH_W), F32),
        scratch_shapes=[pltpu.VMEM((HEADS_PER_TILE, Q_TILE, LANES), F32)],
        compiler_params=pltpu.CompilerParams(dimension_semantics=("arbitrary",) * 3, vmem_limit_bytes=VMEM_LIMIT),
        name="sb_prompt",
    )(q_bf, kt_bf, v_bf, upper2)


def _merge_kernel(x_ref, oa_ref, ob_ref, sga_ref, sgb_ref, g_ref, wa_ref, wb_ref, wo_ref, y_ref):
    a = jnp.dot((oa_ref[...] * sga_ref[...]).astype(BF16), wa_ref[...], preferred_element_type=F32)
    b = jnp.dot((ob_ref[...] * sgb_ref[...]).astype(BF16), wb_ref[...], preferred_element_type=F32)
    g = g_ref[...]
    merged = g[:, :D_MODEL] * a + g[:, D_MODEL:] * b
    y_ref[...] = x_ref[...] + jnp.dot(merged.astype(BF16), wo_ref[...], preferred_element_type=F32)


def _merge(x, oa, ob, sga, sgb, g, wa_bf, wb_bf, wo_bf, *, name):
    n = x.shape[0]
    tm = min(MERGE_ROWS, n)
    row = lambda i: (i, 0)
    const = lambda i: (0, 0)
    wide = pl.BlockSpec((tm, BRANCH_W), row)
    return pl.pallas_call(
        _merge_kernel,
        grid=(n // tm,),
        in_specs=[
            pl.BlockSpec((tm, D_MODEL), row), wide, wide, wide, wide,
            pl.BlockSpec((tm, 2 * D_MODEL), row),
            pl.BlockSpec((BRANCH_W, D_MODEL), const),
            pl.BlockSpec((BRANCH_W, D_MODEL), const),
            pl.BlockSpec((D_MODEL, D_MODEL), const),
        ],
        out_specs=pl.BlockSpec((tm, D_MODEL), row),
        out_shape=jax.ShapeDtypeStruct((n, D_MODEL), F32),
        compiler_params=pltpu.CompilerParams(dimension_semantics=("arbitrary",), vmem_limit_bytes=VMEM_LIMIT),
        name=name,
    )(x, oa, ob, sga, sgb, g, wa_bf, wb_bf, wo_bf)


def _stream_kernel(layer, nb, n_pages, pt_ref, qm_ref, qs_ref, mk_hbm, sk_hbm, sv_hbm, w_ref,
                   ob_ref, s_ref, sel_ref, mk_buf, sk_buf, sv_buf, sems, tail_ref, acc_ref):
    b = pl.program_id(0)
    c = pl.program_id(1)
    n_chunks = n_pages // STREAM_PAGES
    total = nb * n_chunks
    step = b * n_chunks + c
    slot = step % 2
    page_len = sk_buf.shape[-1]
    pages_per_block = MOBA_BLOCK // page_len
    n_blocks = n_pages // pages_per_block

    def copies(bb, cc, slot_):
        out = []
        for g in range(STREAM_PAGES):
            page = pt_ref[bb, n_pages - 1 - (cc * STREAM_PAGES + g)]
            for ci, (hbm, buf) in enumerate(((mk_hbm, mk_buf), (sk_hbm, sk_buf), (sv_hbm, sv_buf))):
                out.append(pltpu.make_async_copy(hbm.at[layer, page], buf.at[slot_, g], sems.at[slot_, ci]))
        return out

    @pl.when(step == 0)
    def _():
        for cp in copies(0, 0, 0):
            cp.start()

    @pl.when(step + 1 < total)
    def _():
        nxt = step + 1
        for cp in copies(nxt // n_chunks, nxt % n_chunks, 1 - slot):
            cp.start()

    for cp in copies(b, c, slot):
        cp.wait()

    @pl.when(c == 0)
    def _():
        tail_ref[...] = jnp.zeros_like(tail_ref)
        acc_ref[...] = jnp.zeros_like(acc_ref)

    def scores(k_page, q_rep):
        prod = (k_page * q_rep).reshape(N_HEADS, HEAD_DIM // 8, 8, page_len)
        return jnp.sum(jnp.sum(prod, axis=1), axis=1)

    pages = range(STREAM_PAGES)
    for g in pages:
        s_ref[0, n_pages - 1 - (c * STREAM_PAGES + g)] = scores(mk_buf[slot, g], qm_ref[0])

    z = jnp.stack([scores(sk_buf[slot, g], qs_ref[0]) for g in pages])
    nlm = _neg_log_sigmoid_neg(z)
    hi, lo = _split2(nlm.reshape(STREAM_PAGES * N_HEADS, page_len))
    later = (jnp.dot(hi, w_ref[...], preferred_element_type=F32)
             + jnp.dot(lo, w_ref[...], preferred_element_type=F32)).reshape(z.shape)
    page_sum = jnp.sum(nlm, axis=-1, keepdims=True)
    tails = [tail_ref[...]]
    for g in pages:
        tails.append(tails[-1] + page_sum[g])
    a = jnp.exp(z - nlm - (jnp.stack(tails[:-1]) + later))
    tail_ref[...] = tails[-1]
    acc = acc_ref[...]
    for g in pages:
        acc = acc + a[g][:, None, :] * sv_buf[slot, g]
    acc_ref[...] = acc

    @pl.when(c == n_chunks - 1)
    def _():
        ob_ref[0] = jnp.sum(acc_ref[...], axis=-1)
        scores = s_ref[0].reshape(n_blocks, pages_per_block, N_HEADS, page_len)
        gate = jnp.sum(jnp.sum(scores, axis=1), axis=-1, keepdims=True) * (1.0 / MOBA_BLOCK)
        gate = jnp.broadcast_to(gate, (n_blocks, N_HEADS, page_len))
        blk = lax.broadcasted_iota(jnp.int32, gate.shape, 0)
        rank = _topk_rank(gate, blk)
        for t in range(MOBA_TOPK):
            sel_ref[0, t] = jnp.sum(jnp.where(rank == float(t), blk.astype(F32), 0.0), axis=0).astype(jnp.int32)


def _decode_stream(layer, page_table, q_moba, q_sb, mk, sk, sv, lower):
    nb, n_pages = page_table.shape
    page_len = mk.shape[-1]
    n_chunks = n_pages // STREAM_PAGES
    any_spec = pl.BlockSpec(memory_space=pl.ANY)
    buf = pltpu.VMEM((2, STREAM_PAGES, N_HEADS, HEAD_DIM, page_len), F32)
    q_spec = pl.BlockSpec((1, N_HEADS, HEAD_DIM, page_len), lambda b, c, pt: (b, 0, 0, 0))
    grid_spec = pltpu.PrefetchScalarGridSpec(
        num_scalar_prefetch=1,
        grid=(nb, n_chunks),
        in_specs=[q_spec, q_spec, any_spec, any_spec, any_spec,
                  pl.BlockSpec((page_len, page_len), lambda b, c, pt: (0, 0))],
        out_specs=(
            pl.BlockSpec((1, N_HEADS, HEAD_DIM), lambda b, c, pt: (b, 0, 0)),
            pl.BlockSpec((1, n_pages, N_HEADS, page_len), lambda b, c, pt: (b, 0, 0, 0)),
            pl.BlockSpec((1, MOBA_TOPK, N_HEADS, page_len), lambda b, c, pt: (b, 0, 0, 0)),
        ),
        scratch_shapes=[buf, buf, buf, pltpu.SemaphoreType.DMA((2, 3)),
                        pltpu.VMEM((N_HEADS, 1), F32), pltpu.VMEM((N_HEADS, HEAD_DIM, page_len), F32)],
    )
    return pl.pallas_call(
        functools.partial(_stream_kernel, layer, nb, n_pages),
        grid_spec=grid_spec,
        out_shape=(jax.ShapeDtypeStruct((nb, N_HEADS, HEAD_DIM), F32),
                   jax.ShapeDtypeStruct((nb, n_pages, N_HEADS, page_len), F32),
                   jax.ShapeDtypeStruct((nb, MOBA_TOPK, N_HEADS, page_len), jnp.int32)),
        compiler_params=pltpu.CompilerParams(dimension_semantics=("arbitrary", "arbitrary"),
                                             vmem_limit_bytes=VMEM_LIMIT),
        name="decode_stream",
    )(page_table, q_moba, q_sb, mk, sk, sv, lower)


def _moba_decode_kernel(layer, nb, pt_ref, selsm_ref, s_ref, selv_ref, q_ref, kn_ref, vn_ref, mv_hbm, o_ref,
                        vbuf, sems, p_ref):
    b = pl.program_id(0)
    slot = b % 2
    n_pages, _, page_len = s_ref.shape[1:]
    pages_per_block = MOBA_BLOCK // page_len

    def tiles():
        return [(h, t, half) for h in range(N_HEADS) for t in range(MOBA_TOPK) for half in range(pages_per_block)]

    def copies(bb, slot_):
        out = []
        for h, t, half in tiles():
            page = pt_ref[bb, selsm_ref[bb, t * N_HEADS + h] * pages_per_block + half]
            out.append(pltpu.make_async_copy(mv_hbm.at[layer, page, h], vbuf.at[slot_, h, t, half], sems.at[slot_]))
        return out

    @pl.when(b == 0)
    def _():
        for cp in copies(0, 0):
            cp.start()

    @pl.when(b + 1 < nb)
    def _():
        for cp in copies(b + 1, 1 - slot):
            cp.start()

    s = s_ref[0]
    page_blk = lax.broadcasted_iota(jnp.int32, s.shape, 0) // pages_per_block
    sm = jnp.full(s.shape, NEG_INF, F32)
    for t in range(MOBA_TOPK):
        sm = jnp.where(page_blk == selv_ref[0, t][None], s, sm)
    s_new = jnp.sum(q_ref[0] * kn_ref[0], axis=-1, keepdims=True)
    m = jnp.maximum(jnp.max(jnp.max(sm, axis=0), axis=-1, keepdims=True), s_new)
    p = jnp.exp(sm - m)
    p_new = jnp.exp(s_new - m)
    l = jnp.sum(jnp.sum(p, axis=0), axis=-1, keepdims=True) + p_new
    p_ref[...] = p

    for cp in copies(b, slot):
        cp.wait()

    head = lax.broadcasted_iota(jnp.int32, (N_HEADS, HEAD_DIM), 0)
    acc = p_new * vn_ref[0]
    for h, t, half in tiles():
        page = selsm_ref[b, t * N_HEADS + h] * pages_per_block + half
        res = lax.dot_general(p_ref[page].astype(BF16), vbuf[slot, h, t, half].astype(BF16), _NT,
                              preferred_element_type=F32)
        acc = acc + jnp.where(head == h, res, 0.0)
    o_ref[0] = acc / l


def _moba_decode(layer, page_table, sel_scalar, scores, sel_vec, q, k_new, v_new, mv):
    nb, n_pages = page_table.shape
    page_len = mv.shape[-1]
    pages_per_block = MOBA_BLOCK // page_len
    any_spec = pl.BlockSpec(memory_space=pl.ANY)
    row = pl.BlockSpec((1, N_HEADS, HEAD_DIM), lambda b, pt, sl: (b, 0, 0))
    grid_spec = pltpu.PrefetchScalarGridSpec(
        num_scalar_prefetch=2,
        grid=(nb,),
        in_specs=[
            pl.BlockSpec((1, n_pages, N_HEADS, page_len), lambda b, pt, sl: (b, 0, 0, 0)),
            pl.BlockSpec((1, MOBA_TOPK, N_HEADS, page_len), lambda b, pt, sl: (b, 0, 0, 0)),
            row, row, row, any_spec,
        ],
        out_specs=row,
        scratch_shapes=[pltpu.VMEM((2, N_HEADS, MOBA_TOPK, pages_per_block, HEAD_DIM, page_len), F32),
                        pltpu.SemaphoreType.DMA((2,)),
                        pltpu.VMEM((n_pages, N_HEADS, page_len), F32)],
    )
    return pl.pallas_call(
        functools.partial(_moba_decode_kernel, layer, nb),
        grid_spec=grid_spec,
        out_shape=jax.ShapeDtypeStruct((nb, N_HEADS, HEAD_DIM), F32),
        compiler_params=pltpu.CompilerParams(dimension_semantics=("arbitrary",), vmem_limit_bytes=VMEM_LIMIT),
        name="moba_decode",
    )(page_table, sel_scalar, scores, sel_vec, q, k_new, v_new, mv)


def _rope_tables(pos):
    half = ROT_DIM // 2
    inv = ROPE_THETA ** (-jnp.arange(half, dtype=F32) * 2.0 / ROT_DIM)
    ang = pos[:, None] * inv[None, :]
    cos, sin = jnp.cos(ang), jnp.sin(ang)
    n = pos.shape[0]
    pad = jnp.zeros((n, HEAD_DIM - ROT_DIM), F32)
    zero = jnp.zeros((n, half), F32)
    cos_h = jnp.concatenate([cos, cos, pad + 1.0], axis=1)
    s1_h = jnp.concatenate([-sin, zero, pad], axis=1)
    s2_h = jnp.concatenate([zero, sin, pad], axis=1)
    return tuple(jnp.tile(t, (1, N_HEADS)) for t in (cos_h, s1_h, s2_h))


def kernel(x_prompt, x_sample, cache_moba_k, cache_moba_v, cache_sb_k, cache_sb_v, page_table,
           norm_g, w_in, b_gate, q_norm_g, k_norm_g, w_branch_a, w_branch_b, w_out):
    depth = w_in.shape[0]
    bsz, seq, _ = x_prompt.shape
    nb, dec_seq, _ = x_sample.shape
    page_len = cache_moba_k.shape[2]
    n_pages = page_table.shape[1]
    past_len = n_pages * page_len
    n_prompt = bsz * seq
    n_qblk = seq // Q_TILE
    assert dec_seq == 1 and seq % Q_TILE == 0 and MOBA_BLOCK % page_len == 0 and n_qblk <= LANES
    assert n_pages % STREAM_PAGES == 0 and past_len % MOBA_BLOCK == 0 and page_len == LANES

    lane_head = jnp.arange(BRANCH_W) // HEAD_DIM
    seg = (lane_head[:, None] == lane_head[None, :]).astype(BF16)
    idx = jnp.arange(Q_TILE)
    later_q = (idx[:, None] > idx[None, :]).astype(BF16)
    later_q2 = jnp.concatenate([later_q, later_q], axis=0)
    later_p = later_q[:page_len, :page_len]
    block_onehot = jnp.broadcast_to(
        (jnp.arange(LANES)[None, :, None] == jnp.arange(n_qblk)[:, None, None]), (n_qblk, LANES, Q_TILE)).astype(BF16)

    cos_p, s1_p, s2_p = _rope_tables(jnp.arange(seq, dtype=F32))
    cos_s, s1_s, s2_s = _rope_tables(jnp.full((nb,), past_len, F32))

    mk_t, mv_t, sk_t, sv_t = (jnp.transpose(c, (0, 1, 3, 4, 2))
                              for c in (cache_moba_k, cache_moba_v, cache_sb_k, cache_sb_v))

    def lane_replicated(q):
        return jnp.broadcast_to(q.reshape(nb, N_HEADS, HEAD_DIM, 1), (nb, N_HEADS, HEAD_DIM, page_len))

    yp = x_prompt.reshape(n_prompt, D_MODEL)
    ys = x_sample.reshape(nb, D_MODEL)
    kv_prompt = ()
    rows_s = [[] for _ in range(4)]
    for l in range(depth):
        w_bf = w_in[l].astype(BF16)
        wa_bf, wb_bf, wo_bf = (w[l].astype(BF16) for w in (w_branch_a, w_branch_b, w_out))
        params = (norm_g[l][None, :], w_bf, b_gate[l][None, :],
                  jnp.tile(q_norm_g[l], N_HEADS)[None, :], jnp.tile(k_norm_g[l], N_HEADS)[None, :])

        (qa, kat_bf, va_bf, qb, kbt_bf, vb_bf, *kv_prompt, sga, sgb, g, kmean) = _proj(
            yp, *params, cos_p, s1_p, s2_p, seg, prompt=True, seq=seq, layer=l, depth=depth, carried=kv_prompt)
        as3 = lambda a: a.reshape(bsz, seq, BRANCH_W)
        kt4 = lambda a: a.reshape(bsz, n_qblk, BRANCH_W, Q_TILE)
        oa = _moba_prompt(as3(qa), kt4(kat_bf), as3(va_bf), kmean.reshape(bsz, n_qblk, BRANCH_W), block_onehot)
        ob = _sb_prompt(as3(qb), kt4(kbt_bf), as3(vb_bf), later_q2)
        yp = _merge(yp, oa.reshape(n_prompt, BRANCH_W), ob.reshape(n_prompt, BRANCH_W), sga, sgb, g,
                    wa_bf, wb_bf, wo_bf, name="merge_prompt")

        (qa_s, ka_s, va_s, qb_s, kb_s, vb_s, sga_s, sgb_s, g_s) = _proj(
            ys, *params, cos_s, s1_s, s2_s, seg, prompt=False)
        heads = lambda a: a.reshape(nb, N_HEADS, HEAD_DIM)
        ob_s, scores, sel_vec = _decode_stream(l, page_table, lane_replicated(qa_s), lane_replicated(qb_s),
                                               mk_t, sk_t, sv_t, later_p)
        sel_scalar = sel_vec[:, :, :, 0].reshape(nb, MOBA_TOPK * N_HEADS)
        oa_s = _moba_decode(l, page_table, sel_scalar, scores, sel_vec, heads(qa_s), heads(ka_s), heads(va_s), mv_t)
        ys = _merge(ys, oa_s.reshape(nb, BRANCH_W), ob_s.reshape(nb, BRANCH_W), sga_s, sgb_s, g_s,
                    wa_bf, wb_bf, wo_bf, name="merge_sample")
        for dst, src in zip(rows_s, (ka_s, va_s, kb_s, vb_s)):
            dst.append(src.reshape(nb, 1, N_HEADS, HEAD_DIM))

    rows_p = [jnp.transpose(a.reshape(depth, bsz, N_HEADS, HEAD_DIM, seq), (0, 1, 4, 2, 3)) for a in kv_prompt]
    return (yp.reshape(bsz, seq, D_MODEL), ys.reshape(nb, 1, D_MODEL),
            *rows_p, *(jnp.stack(r) for r in rows_s))
```

```python
import functools
import math

import jax
import jax.numpy as jnp
import numpy as np
from jax import lax
from jax.experimental import pallas as pl
from jax.experimental.pallas import tpu as pltpu

F32 = jnp.float32
BF16 = jnp.bfloat16

D_MODEL = 1024
HEAD_DIM = 64
N_HEADS = 8
BRANCH_W = N_HEADS * HEAD_DIM
ROT_DIM = HEAD_DIM // 4
ROPE_THETA = 500000.0
MOBA_BLOCK = 256
MOBA_TOPK = 3
EPS = 1e-6
NEG_INF = -1e30
QK_SCALE = 1.0 / math.sqrt(HEAD_DIM)
SB_DEAD_TAIL = 104.0

LANES = 128
HEADS_PER_TILE = LANES // HEAD_DIM
N_PAIRS = N_HEADS // HEADS_PER_TILE
Q_TILE = MOBA_BLOCK
MOBA_GROUP = 4
PROJ_ROWS = 256
MERGE_ROWS = 512
STREAM_PAGES = 16
VMEM_LIMIT = 56 * 1024 * 1024

_NT = (((1,), (1,)), ((), ()))


def _split3(x):
    a = x.astype(BF16)
    r = x - a.astype(F32)
    b = r.astype(BF16)
    c = (r - b.astype(F32)).astype(BF16)
    return a, b, c


def _split2(x):
    a = x.astype(BF16)
    b = (x - a.astype(F32)).astype(BF16)
    return a, b


def _sigmoid(x):
    return 1.0 / (1.0 + jnp.exp(-x))


def _neg_log_sigmoid_neg(z):
    return jnp.maximum(z, 0.0) + jnp.log(1.0 + jnp.exp(-jnp.abs(z)))


def _topk_rank(gate, axis_index):
    rank = jnp.zeros(gate.shape, F32)
    for jp in range(gate.shape[0]):
        row = gate[jp:jp + 1]
        rank = rank + jnp.where(row > gate, 1.0,
                                jnp.where(row == gate, jnp.where(jp < axis_index, 1.0, 0.0), 0.0))
    return rank


def _proj_kernel(prompt, n_carried, x_ref, ng_ref, w_ref, bg_ref, qg_ref, kg_ref, cos_ref, s1_ref, s2_ref,
                 seg_ref, *rest):
    outs = rest[n_carried:]
    x = x_ref[...]
    ms = jnp.mean(x * x, axis=-1, keepdims=True)
    h = (x * lax.rsqrt(ms + EPS) * ng_ref[...]).astype(BF16)

    def col(c):
        return jnp.dot(h, w_ref[:, c * BRANCH_W:(c + 1) * BRANCH_W], preferred_element_type=F32)

    def all_heads(table_ref):
        return jnp.concatenate([table_ref[...]] * N_PAIRS, axis=1)

    cos, s1, s2 = all_heads(cos_ref), all_heads(s1_ref), all_heads(s2_ref)

    def head_norm_rope(u, g):
        ss = jnp.dot((u * u).astype(BF16), seg_ref[...], preferred_element_type=F32)
        y = u * lax.rsqrt(ss * (1.0 / HEAD_DIM) + EPS) * g
        return y * cos + pltpu.roll(y, BRANCH_W - ROT_DIM // 2, 1) * s1 + pltpu.roll(y, ROT_DIM // 2, 1) * s2

    def silu(u):
        return u * _sigmoid(u)

    qa = head_norm_rope(col(0), qg_ref[...]) * QK_SCALE
    ka = head_norm_rope(col(1), kg_ref[...])
    va = col(2)
    sga = silu(col(3))
    qb = col(4) * QK_SCALE
    kb = col(5)
    vb = col(6)
    sgb = silu(col(7))
    gm = jnp.dot(h, w_ref[:, 8 * BRANCH_W:], preferred_element_type=F32) + bg_ref[...]
    g = _sigmoid(gm)

    if prompt:
        (qa_ref, katbf_ref, vabf_ref, qb_ref, kbtbf_ref, vbbf_ref, kat_ref, vat_ref, kbt_ref, vbt_ref,
         sga_ref, sgb_ref, g_ref, kmean_ref) = outs
        qa_ref[...] = qa.astype(BF16)
        qb_ref[...] = qb.astype(BF16)
        vabf_ref[...] = va.astype(BF16)
        vbbf_ref[...] = vb.astype(BF16)
        ka_t, kb_t = ka.T, kb.T
        katbf_ref[0] = ka_t.astype(BF16)
        kbtbf_ref[0] = kb_t.astype(BF16)
        for ref, val in ((kat_ref, ka_t), (vat_ref, va.T), (kbt_ref, kb_t), (vbt_ref, vb.T)):
            for d in range(ref.shape[0]):
                ref[d, 0] = val
        kmean_ref[0] = jnp.mean(ka, axis=0, keepdims=True)
    else:
        qa_ref, ka_ref, va_ref, qb_ref, kb_ref, vb_ref, sga_ref, sgb_ref, g_ref = outs
        qa_ref[...] = qa
        qb_ref[...] = qb
        ka_ref[...] = ka
        va_ref[...] = va
        kb_ref[...] = kb
        vb_ref[...] = vb
    sga_ref[...] = sga
    sgb_ref[...] = sgb
    g_ref[...] = g


def _proj(x, ng, w_bf, bg, qg, kg, cos, s1, s2, seg, *, prompt, seq=None, layer=0, depth=1, carried=()):
    n = x.shape[0]
    tm = PROJ_ROWS if prompt else n
    nt = n // tm
    in_cols = w_bf.shape[1]
    row = lambda i: (i, 0)
    const = lambda i: (0, 0)
    wide = pl.BlockSpec((tm, BRANCH_W), row)
    tab_blocks = cos.shape[0] // tm
    table = pl.BlockSpec((tm, LANES), lambda i: (i % tab_blocks, 0))
    in_specs = [
        pl.BlockSpec((tm, D_MODEL), row),
        pl.BlockSpec((1, D_MODEL), const),
        pl.BlockSpec((D_MODEL, in_cols), const),
        pl.BlockSpec((1, 2 * D_MODEL), const),
        pl.BlockSpec((1, BRANCH_W), const),
        pl.BlockSpec((1, BRANCH_W), const),
        table, table, table,
        pl.BlockSpec((BRANCH_W, BRANCH_W), const),
    ]
    f32w = jax.ShapeDtypeStruct((n, BRANCH_W), F32)
    bf16w = jax.ShapeDtypeStruct((n, BRANCH_W), BF16)
    gate = jax.ShapeDtypeStruct((n, 2 * D_MODEL), F32)
    gate_spec = pl.BlockSpec((tm, 2 * D_MODEL), row)
    if prompt:
        tiles_per_seq = seq // tm
        ktbf = jax.ShapeDtypeStruct((nt, BRANCH_W, tm), BF16)
        ktbf_spec = pl.BlockSpec((1, BRANCH_W, tm), lambda i: (i, 0, 0))
        kvt = jax.ShapeDtypeStruct((depth, n // seq, BRANCH_W, seq), F32)
        slabs = depth if layer == 0 else 1
        kvt_spec = pl.BlockSpec((slabs, 1, BRANCH_W, tm),
                                lambda i: (layer, i // tiles_per_seq, 0, i % tiles_per_seq))
        kmean = jax.ShapeDtypeStruct((nt, 1, BRANCH_W), F32)
        kmean_spec = pl.BlockSpec((1, 1, BRANCH_W), lambda i: (i, 0, 0))
        out_shape = (bf16w, ktbf, bf16w, bf16w, ktbf, bf16w, kvt, kvt, kvt, kvt, f32w, f32w, gate, kmean)
        out_specs = (wide, ktbf_spec, wide, wide, ktbf_spec, wide, kvt_spec, kvt_spec, kvt_spec, kvt_spec,
                     wide, wide, gate_spec, kmean_spec)
    else:
        out_shape = (f32w,) * 8 + (gate,)
        out_specs = (wide,) * 8 + (gate_spec,)
    first_kv_out = 6
    aliases = {len(in_specs) + k: first_kv_out + k for k in range(len(carried))}
    in_specs += [pl.BlockSpec(memory_space=pl.ANY)] * len(carried)
    return pl.pallas_call(
        functools.partial(_proj_kernel, prompt, len(carried)),
        grid=(nt,),
        in_specs=in_specs,
        out_specs=out_specs,
        out_shape=out_shape,
        input_output_aliases=aliases,
        compiler_params=pltpu.CompilerParams(dimension_semantics=("arbitrary",), vmem_limit_bytes=VMEM_LIMIT),
        name="proj_prompt" if prompt else "proj_sample",
    )(x, ng, w_bf, bg, qg, kg, cos, s1, s2, seg, *carried)


def _head_masked(q, hh):
    lane = lax.broadcasted_iota(jnp.int32, (1, LANES), 1)
    keep = jnp.where((lane // HEAD_DIM) == hh, 1.0, 0.0)
    return (q.astype(F32) * keep).astype(BF16)


def _pick_head_lanes(per_head):
    lane = lax.broadcasted_iota(jnp.int32, (1, LANES), 1)
    out = per_head[-1]
    for hh in range(HEADS_PER_TILE - 2, -1, -1):
        out = jnp.where(lane < (hh + 1) * HEAD_DIM, per_head[hh], out)
    return out


def _attn_specs(b, t, nblk):
    in_specs = [
        pl.BlockSpec((1, Q_TILE, LANES), lambda bi, p, i: (bi, i, p)),
        pl.BlockSpec((1, nblk, LANES, Q_TILE), lambda bi, p, i: (bi, 0, p, 0)),
        pl.BlockSpec((1, t, LANES), lambda bi, p, i: (bi, 0, p)),
    ]
    out_spec = pl.BlockSpec((1, Q_TILE, LANES), lambda bi, p, i: (bi, i, p))
    return in_specs, out_spec


def _key_block(kt_ref, v_ref, j):
    return kt_ref[0, j], v_ref[0, pl.ds(pl.multiple_of(j * Q_TILE, Q_TILE), Q_TILE), :]


def _moba_kernel(q_ref, kt_ref, v_ref, km_ref, e_ref, o_ref, acc_ref):
    i = pl.program_id(2)
    nblk = km_ref.shape[1]
    q = q_ref[0]
    km_parts = _split3(km_ref[0])
    blk = lax.broadcasted_iota(jnp.int32, (nblk, Q_TILE), 0)
    past = blk < i
    r = lax.broadcasted_iota(jnp.int32, (Q_TILE, Q_TILE), 0)
    c = lax.broadcasted_iota(jnp.int32, (Q_TILE, Q_TILE), 1)
    causal = c <= r

    heads = range(HEADS_PER_TILE)
    lane = lax.broadcasted_iota(jnp.int32, (1, LANES), 1)
    in_head = [(lane // HEAD_DIM) == hh for hh in heads]

    def v_with_ones(v, hh):
        return jnp.where(jnp.broadcast_to(in_head[hh], v.shape), v, jnp.ones_like(v))

    kt_own, v_own = _key_block(kt_ref, v_ref, i)
    qh, ms = [], []
    for hh in heads:
        qm = _head_masked(q, hh)
        gate_t = sum(lax.dot_general(p, qm, _NT, preferred_element_type=F32) for p in km_parts)
        rank = _topk_rank(jnp.where(past, gate_t, NEG_INF), blk)
        pen_t = jnp.where(past, jnp.where(rank < MOBA_TOPK, 0.0, NEG_INF), NEG_INF)
        pen_t = jnp.concatenate([pen_t, jnp.zeros((LANES - nblk, Q_TILE), F32)], axis=0)
        qh.append(jnp.concatenate([qm, pen_t.T.astype(BF16)], axis=1))
        s = jnp.dot(qm, kt_own, preferred_element_type=F32)
        s = jnp.where(causal, s, NEG_INF)
        m = jnp.max(s, axis=-1, keepdims=True)
        acc_ref[hh] = jnp.dot(jnp.exp(s - m).astype(BF16), v_with_ones(v_own, hh), preferred_element_type=F32)
        ms.append(m)

    def attend(j, width, ms):
        kts = [jnp.concatenate([kt_ref[0, j + w], e_ref[j + w]], axis=0) for w in range(width)]
        v = v_ref[0, pl.ds(pl.multiple_of(j * Q_TILE, Q_TILE), width * Q_TILE), :]
        scores = [[jnp.dot(qh[hh], kt, preferred_element_type=F32) for kt in kts] for hh in heads]
        out = []
        for hh in heads:
            s = scores[hh][0] if width == 1 else jnp.concatenate(scores[hh], axis=1)
            m_new = jnp.maximum(ms[hh], jnp.max(s, axis=-1, keepdims=True))
            p = jnp.exp(s - m_new)
            acc_ref[hh] = (jnp.exp(ms[hh] - m_new) * acc_ref[hh]
                           + jnp.dot(p.astype(BF16), v_with_ones(v, hh), preferred_element_type=F32))
            out.append(m_new)
        return tuple(out)

    ms = lax.fori_loop(0, i // MOBA_GROUP, lambda it, ms: attend(MOBA_GROUP * it, MOBA_GROUP, ms), tuple(ms))
    done = (i // MOBA_GROUP) * MOBA_GROUP
    width = MOBA_GROUP // 2
    while width >= 1:
        take = ((i - done) // width) % 2 == 1
        ms = lax.cond(take, functools.partial(attend, done, width), lambda ms: ms, ms)
        done = done + jnp.where(take, width, 0)
        width //= 2

    outs = []
    for hh in heads:
        acc = acc_ref[hh]
        row_sum = jnp.max(jnp.where(in_head[hh], 0.0, acc), axis=-1, keepdims=True)
        outs.append(acc / row_sum)
    o_ref[0] = _pick_head_lanes(outs)


def _moba_prompt(q_bf, kt_bf, v_bf, kmean, block_onehot):
    b, t, _ = q_bf.shape
    nblk = t // Q_TILE
    in_specs, out_spec = _attn_specs(b, t, nblk)
    in_specs += [
        pl.BlockSpec((1, nblk, LANES), lambda bi, p, i: (bi, 0, p)),
        pl.BlockSpec((nblk, LANES, Q_TILE), lambda bi, p, i: (0, 0, 0)),
    ]
    return pl.pallas_call(
        _moba_kernel,
        grid=(b, N_PAIRS, nblk),
        in_specs=in_specs,
        out_specs=out_spec,
        out_shape=jax.ShapeDtypeStruct((b, t, BRANCH_W), F32),
        scratch_shapes=[pltpu.VMEM((HEADS_PER_TILE, Q_TILE, LANES), F32)],
        compiler_params=pltpu.CompilerParams(dimension_semantics=("arbitrary",) * 3, vmem_limit_bytes=VMEM_LIMIT),
        name="moba_prompt",
    )(q_bf, kt_bf, v_bf, kmean, block_onehot)


def _sb_kernel(q_ref, kt_ref, v_ref, u2_ref, o_ref, acc_ref):
    i = pl.program_id(2)
    q = q_ref[0]
    r = lax.broadcasted_iota(jnp.int32, (Q_TILE, Q_TILE), 0)
    c = lax.broadcasted_iota(jnp.int32, (Q_TILE, Q_TILE), 1)
    strict = c < r
    qh = [_head_masked(q, hh) for hh in range(HEADS_PER_TILE)]

    heads = range(HEADS_PER_TILE)

    def span(j, width, ends_with_own, tails):
        kts = [kt_ref[0, j + w] for w in range(width)]
        v = v_ref[0, pl.ds(pl.multiple_of(j * Q_TILE, Q_TILE), width * Q_TILE), :]
        zs = [[jnp.dot(qh[hh], kt, preferred_element_type=F32) for kt in kts] for hh in heads]
        out = []
        for hh in heads:
            seen = tails[hh]
            weights = [None] * width
            for w in reversed(range(width)):
                own = ends_with_own and w == width - 1
                z = zs[hh][w]
                nlm = _neg_log_sigmoid_neg(z)
                if own:
                    nlm = jnp.where(strict, nlm, 0.0)
                later = jnp.dot(jnp.concatenate(_split2(nlm), axis=1), u2_ref[...], preferred_element_type=F32)
                a = jnp.exp(z - nlm - (seen + later))
                if own:
                    a = jnp.where(strict, a, 0.0)
                weights[w] = a.astype(BF16)
                seen = seen + jnp.sum(nlm, axis=-1, keepdims=True)
            a_all = weights[0] if width == 1 else jnp.concatenate(weights, axis=1)
            acc_ref[hh] += jnp.dot(a_all, v, preferred_element_type=F32)
            out.append(seen)
        return tuple(out)

    acc_ref[...] = jnp.zeros_like(acc_ref)
    fresh = (jnp.zeros((Q_TILE, 1), F32),) * HEADS_PER_TILE
    tails = lax.cond(i >= 1, lambda: span(i - 1, 2, True, fresh), lambda: span(i, 1, True, fresh))

    def live(carry):
        j, tails = carry[0], carry[1:]
        lowest = functools.reduce(jnp.minimum, tails)
        return jnp.logical_and(j >= 0, jnp.min(lowest) < SB_DEAD_TAIL)

    def step(carry):
        j, tails = carry[0], carry[1:]
        return (j - 1,) + span(j, 1, False, tails)

    lax.while_loop(live, step, (i - 2,) + tails)
    o_ref[0] = _pick_head_lanes([acc_ref[hh] for hh in range(HEADS_PER_TILE)])


def _sb_prompt(q_bf, kt_bf, v_bf, upper2):
    b, t, _ = q_bf.shape
    nblk = t // Q_TILE
    in_specs, out_spec = _attn_specs(b, t, nblk)
    in_specs.append(pl.BlockSpec((2 * Q_TILE, Q_TILE), lambda bi, p, i: (0, 0)))
    return pl.pallas_call(
        _sb_kernel,
        grid=(b, N_PAIRS, nblk),
        in_specs=in_specs,
        out_specs=out_spec,
        out_shape=jax.ShapeDtypeStruct((b, t, BRANCH_W), F32),
        scratch_shapes=[pltpu.VMEM((HEADS_PER_TILE, Q_TILE, LANES), F32)],
        compiler_params=pltpu.CompilerParams(dimension_semantics=("arbitrary",) * 3, vmem_limit_bytes=VMEM_LIMIT),
        name="sb_prompt",
    )(q_bf, kt_bf, v_bf, upper2)


def _merge_kernel(x_ref, oa_ref, ob_ref, sga_ref, sgb_ref, g_ref, wa_ref, wb_ref, wo_ref, y_ref):
    a = jnp.dot((oa_ref[...] * sga_ref[...]).astype(BF16), wa_ref[...], preferred_element_type=F32)
    b = jnp.dot((ob_ref[...] * sgb_ref[...]).astype(BF16), wb_ref[...], preferred_element_type=F32)
    g = g_ref[...]
    merged = g[:, :D_MODEL] * a + g[:, D_MODEL:] * b
    y_ref[...] = x_ref[...] + jnp.dot(merged.astype(BF16), wo_ref[...], preferred_element_type=F32)


def _merge(x, oa, ob, sga, sgb, g, wa_bf, wb_bf, wo_bf, *, name):
    n = x.shape[0]
    tm = min(MERGE_ROWS, n)
    row = lambda i: (i, 0)
    const = lambda i: (0, 0)
    wide = pl.BlockSpec((tm, BRANCH_W), row)
    return pl.pallas_call(
        _merge_kernel,
        grid=(n // tm,),
        in_specs=[
            pl.BlockSpec((tm, D_MODEL), row), wide, wide, wide, wide,
            pl.BlockSpec((tm, 2 * D_MODEL), row),
            pl.BlockSpec((BRANCH_W, D_MODEL), const),
            pl.BlockSpec((BRANCH_W, D_MODEL), const),
            pl.BlockSpec((D_MODEL, D_MODEL), const),
        ],
        out_specs=pl.BlockSpec((tm, D_MODEL), row),
        out_shape=jax.ShapeDtypeStruct((n, D_MODEL), F32),
        compiler_params=pltpu.CompilerParams(dimension_semantics=("arbitrary",), vmem_limit_bytes=VMEM_LIMIT),
        name=name,
    )(x, oa, ob, sga, sgb, g, wa_bf, wb_bf, wo_bf)


def _stream_kernel(layer, nb, n_pages, pt_ref, qm_ref, qs_ref, mk_hbm, sk_hbm, sv_hbm, w_ref,
                   ob_ref, s_ref, sel_ref, mk_buf, sk_buf, sv_buf, sems, tail_ref, acc_ref):
    b = pl.program_id(0)
    c = pl.program_id(1)
    n_chunks = n_pages // STREAM_PAGES
    total = nb * n_chunks
    step = b * n_chunks + c
    slot = step % 2
    page_len = sk_buf.shape[-1]
    pages_per_block = MOBA_BLOCK // page_len
    n_blocks = n_pages // pages_per_block

    def copies(bb, cc, slot_):
        out = []
        for g in range(STREAM_PAGES):
            page = pt_ref[bb, n_pages - 1 - (cc * STREAM_PAGES + g)]
            for ci, (hbm, buf) in enumerate(((mk_hbm, mk_buf), (sk_hbm, sk_buf), (sv_hbm, sv_buf))):
                out.append(pltpu.make_async_copy(hbm.at[layer, page], buf.at[slot_, g], sems.at[slot_, ci]))
        return out

    @pl.when(step == 0)
    def _():
        for cp in copies(0, 0, 0):
            cp.start()

    @pl.when(step + 1 < total)
    def _():
        nxt = step + 1
        for cp in copies(nxt // n_chunks, nxt % n_chunks, 1 - slot):
            cp.start()

    for cp in copies(b, c, slot):
        cp.wait()

    @pl.when(c == 0)
    def _():
        tail_ref[...] = jnp.zeros_like(tail_ref)
        acc_ref[...] = jnp.zeros_like(acc_ref)

    def scores(k_page, q_rep):
        prod = (k_page * q_rep).reshape(N_HEADS, HEAD_DIM // 8, 8, page_len)
        return jnp.sum(jnp.sum(prod, axis=1), axis=1)

    pages = range(STREAM_PAGES)
    for g in pages:
        s_ref[0, n_pages - 1 - (c * STREAM_PAGES + g)] = scores(mk_buf[slot, g], qm_ref[0])

    z = jnp.stack([scores(sk_buf[slot, g], qs_ref[0]) for g in pages])
    nlm = _neg_log_sigmoid_neg(z)
    hi, lo = _split2(nlm.reshape(STREAM_PAGES * N_HEADS, page_len))
    later = (jnp.dot(hi, w_ref[...], preferred_element_type=F32)
             + jnp.dot(lo, w_ref[...], preferred_element_type=F32)).reshape(z.shape)
    page_sum = jnp.sum(nlm, axis=-1, keepdims=True)
    tails = [tail_ref[...]]
    for g in pages:
        tails.append(tails[-1] + page_sum[g])
    a = jnp.exp(z - nlm - (jnp.stack(tails[:-1]) + later))
    tail_ref[...] = tails[-1]
    acc = acc_ref[...]
    for g in pages:
        acc = acc + a[g][:, None, :] * sv_buf[slot, g]
    acc_ref[...] = acc

    @pl.when(c == n_chunks - 1)
    def _():
        ob_ref[0] = jnp.sum(acc_ref[...], axis=-1)
        scores = s_ref[0].reshape(n_blocks, pages_per_block, N_HEADS, page_len)
        gate = jnp.sum(jnp.sum(scores, axis=1), axis=-1, keepdims=True) * (1.0 / MOBA_BLOCK)
        gate = jnp.broadcast_to(gate, (n_blocks, N_HEADS, page_len))
        blk = lax.broadcasted_iota(jnp.int32, gate.shape, 0)
        rank = _topk_rank(gate, blk)
        for t in range(MOBA_TOPK):
            sel_ref[0, t] = jnp.sum(jnp.where(rank == float(t), blk.astype(F32), 0.0), axis=0).astype(jnp.int32)


def _decode_stream(layer, page_table, q_moba, q_sb, mk, sk, sv, lower):
    nb, n_pages = page_table.shape
    page_len = mk.shape[-1]
    n_chunks = n_pages // STREAM_PAGES
    any_spec = pl.BlockSpec(memory_space=pl.ANY)
    buf = pltpu.VMEM((2, STREAM_PAGES, N_HEADS, HEAD_DIM, page_len), F32)
    q_spec = pl.BlockSpec((1, N_HEADS, HEAD_DIM, page_len), lambda b, c, pt: (b, 0, 0, 0))
    grid_spec = pltpu.PrefetchScalarGridSpec(
        num_scalar_prefetch=1,
        grid=(nb, n_chunks),
        in_specs=[q_spec, q_spec, any_spec, any_spec, any_spec,
                  pl.BlockSpec((page_len, page_len), lambda b, c, pt: (0, 0))],
        out_specs=(
            pl.BlockSpec((1, N_HEADS, HEAD_DIM), lambda b, c, pt: (b, 0, 0)),
            pl.BlockSpec((1, n_pages, N_HEADS, page_len), lambda b, c, pt: (b, 0, 0, 0)),
            pl.BlockSpec((1, MOBA_TOPK, N_HEADS, page_len), lambda b, c, pt: (b, 0, 0, 0)),
        ),
        scratch_shapes=[buf, buf, buf, pltpu.SemaphoreType.DMA((2, 3)),
                        pltpu.VMEM((N_HEADS, 1), F32), pltpu.VMEM((N_HEADS, HEAD_DIM, page_len), F32)],
    )
    return pl.pallas_call(
        functools.partial(_stream_kernel, layer, nb, n_pages),
        grid_spec=grid_spec,
        out_shape=(jax.ShapeDtypeStruct((nb, N_HEADS, HEAD_DIM), F32),
                   jax.ShapeDtypeStruct((nb, n_pages, N_HEADS, page_len), F32),
                   jax.ShapeDtypeStruct((nb, MOBA_TOPK, N_HEADS, page_len), jnp.int32)),
        compiler_params=pltpu.CompilerParams(dimension_semantics=("arbitrary", "arbitrary"),
                                             vmem_limit_bytes=VMEM_LIMIT),
        name="decode_stream",
    )(page_table, q_moba, q_sb, mk, sk, sv, lower)


def _moba_decode_kernel(layer, nb, pt_ref, selsm_ref, s_ref, selv_ref, q_ref, kn_ref, vn_ref, mv_hbm, o_ref,
                        vbuf, sems, p_ref):
    b = pl.program_id(0)
    slot = b % 2
    n_pages, _, page_len = s_ref.shape[1:]
    pages_per_block = MOBA_BLOCK // page_len

    def tiles():
        return [(h, t, half) for h in range(N_HEADS) for t in range(MOBA_TOPK) for half in range(pages_per_block)]

    def copies(bb, slot_):
        out = []
        for h, t, half in tiles():
            page = pt_ref[bb, selsm_ref[bb, t * N_HEADS + h] * pages_per_block + half]
            out.append(pltpu.make_async_copy(mv_hbm.at[layer, page, h], vbuf.at[slot_, h, t, half], sems.at[slot_]))
        return out

    @pl.when(b == 0)
    def _():
        for cp in copies(0, 0):
            cp.start()

    @pl.when(b + 1 < nb)
    def _():
        for cp in copies(b + 1, 1 - slot):
            cp.start()

    s = s_ref[0]
    page_blk = lax.broadcasted_iota(jnp.int32, s.shape, 0) // pages_per_block
    sm = jnp.full(s.shape, NEG_INF, F32)
    for t in range(MOBA_TOPK):
        sm = jnp.where(page_blk == selv_ref[0, t][None], s, sm)
    s_new = jnp.sum(q_ref[0] * kn_ref[0], axis=-1, keepdims=True)
    m = jnp.maximum(jnp.max(jnp.max(sm, axis=0), axis=-1, keepdims=True), s_new)
    p = jnp.exp(sm - m)
    p_new = jnp.exp(s_new - m)
    l = jnp.sum(jnp.sum(p, axis=0), axis=-1, keepdims=True) + p_new
    p_ref[...] = p

    for cp in copies(b, slot):
        cp.wait()

    head = lax.broadcasted_iota(jnp.int32, (N_HEADS, HEAD_DIM), 0)
    acc = p_new * vn_ref[0]
    for h, t, half in tiles():
        page = selsm_ref[b, t * N_HEADS + h] * pages_per_block + half
        res = lax.dot_general(p_ref[page].astype(BF16), vbuf[slot, h, t, half].astype(BF16), _NT,
                              preferred_element_type=F32)
        acc = acc + jnp.where(head == h, res, 0.0)
    o_ref[0] = acc / l


def _moba_decode(layer, page_table, sel_scalar, scores, sel_vec, q, k_new, v_new, mv):
    nb, n_pages = page_table.shape
    page_len = mv.shape[-1]
    pages_per_block = MOBA_BLOCK // page_len
    any_spec = pl.BlockSpec(memory_space=pl.ANY)
    row = pl.BlockSpec((1, N_HEADS, HEAD_DIM), lambda b, pt, sl: (b, 0, 0))
    grid_spec = pltpu.PrefetchScalarGridSpec(
        num_scalar_prefetch=2,
        grid=(nb,),
        in_specs=[
            pl.BlockSpec((1, n_pages, N_HEADS, page_len), lambda b, pt, sl: (b, 0, 0, 0)),
            pl.BlockSpec((1, MOBA_TOPK, N_HEADS, page_len), lambda b, pt, sl: (b, 0, 0, 0)),
            row, row, row, any_spec,
        ],
        out_specs=row,
        scratch_shapes=[pltpu.VMEM((2, N_HEADS, MOBA_TOPK, pages_per_block, HEAD_DIM, page_len), F32),
                        pltpu.SemaphoreType.DMA((2,)),
                        pltpu.VMEM((n_pages, N_HEADS, page_len), F32)],
    )
    return pl.pallas_call(
        functools.partial(_moba_decode_kernel, layer, nb),
        grid_spec=grid_spec,
        out_shape=jax.ShapeDtypeStruct((nb, N_HEADS, HEAD_DIM), F32),
        compiler_params=pltpu.CompilerParams(dimension_semantics=("arbitrary",), vmem_limit_bytes=VMEM_LIMIT),
        name="moba_decode",
    )(page_table, sel_scalar, scores, sel_vec, q, k_new, v_new, mv)


def _rope_tables(pos):
    half = ROT_DIM // 2
    inv = ROPE_THETA ** (-jnp.arange(half, dtype=F32) * 2.0 / ROT_DIM)
    ang = pos[:, None] * inv[None, :]
    cos, sin = jnp.cos(ang), jnp.sin(ang)
    n = pos.shape[0]
    pad = jnp.zeros((n, HEAD_DIM - ROT_DIM), F32)
    zero = jnp.zeros((n, half), F32)
    cos_h = jnp.concatenate([cos, cos, pad + 1.0], axis=1)
    s1_h = jnp.concatenate([-sin, zero, pad], axis=1)
    s2_h = jnp.concatenate([zero, sin, pad], axis=1)
    return tuple(jnp.tile(t, (1, HEADS_PER_TILE)) for t in (cos_h, s1_h, s2_h))


def kernel(x_prompt, x_sample, cache_moba_k, cache_moba_v, cache_sb_k, cache_sb_v, page_table,
           norm_g, w_in, b_gate, q_norm_g, k_norm_g, w_branch_a, w_branch_b, w_out):
    depth = w_in.shape[0]
    bsz, seq, _ = x_prompt.shape
    nb, dec_seq, _ = x_sample.shape
    page_len = cache_moba_k.shape[2]
    n_pages = page_table.shape[1]
    past_len = n_pages * page_len
    n_prompt = bsz * seq
    n_qblk = seq // Q_TILE
    assert dec_seq == 1 and seq % Q_TILE == 0 and MOBA_BLOCK % page_len == 0 and n_qblk <= LANES
    assert n_pages % STREAM_PAGES == 0 and past_len % MOBA_BLOCK == 0 and page_len == LANES

    lane_head = np.arange(BRANCH_W) // HEAD_DIM
    seg = (lane_head[:, None] == lane_head[None, :]).astype(BF16)
    idx = np.arange(Q_TILE)
    later_q = (idx[:, None] > idx[None, :]).astype(BF16)
    later_q2 = np.concatenate([later_q, later_q], axis=0)
    later_p = later_q[:page_len, :page_len]
    block_onehot = np.broadcast_to(
        (np.arange(LANES)[None, :, None] == np.arange(n_qblk)[:, None, None]), (n_qblk, LANES, Q_TILE)).astype(BF16)

    cos_p, s1_p, s2_p = _rope_tables(jnp.arange(seq, dtype=F32))
    cos_s, s1_s, s2_s = _rope_tables(jnp.full((nb,), past_len, F32))

    mk_t, mv_t, sk_t, sv_t = (jnp.transpose(c, (0, 1, 3, 4, 2))
                              for c in (cache_moba_k, cache_moba_v, cache_sb_k, cache_sb_v))

    def lane_replicated(q):
        return jnp.broadcast_to(q.reshape(nb, N_HEADS, HEAD_DIM, 1), (nb, N_HEADS, HEAD_DIM, page_len))

    yp = x_prompt.reshape(n_prompt, D_MODEL)
    ys = x_sample.reshape(nb, D_MODEL)
    kv_prompt = ()
    rows_s = [[] for _ in range(4)]
    for l in range(depth):
        w_bf = w_in[l].astype(BF16)
        wa_bf, wb_bf, wo_bf = (w[l].astype(BF16) for w in (w_branch_a, w_branch_b, w_out))
        params = (norm_g[l][None, :], w_bf, b_gate[l][None, :],
                  jnp.tile(q_norm_g[l], N_HEADS)[None, :], jnp.tile(k_norm_g[l], N_HEADS)[None, :])

        (qa, kat_bf, va_bf, qb, kbt_bf, vb_bf, *kv_prompt, sga, sgb, g, kmean) = _proj(
            yp, *params, cos_p, s1_p, s2_p, seg, prompt=True, seq=seq, layer=l, depth=depth, carried=kv_prompt)
        as3 = lambda a: a.reshape(bsz, seq, BRANCH_W)
        kt4 = lambda a: a.reshape(bsz, n_qblk, BRANCH_W, Q_TILE)
        oa = _moba_prompt(as3(qa), kt4(kat_bf), as3(va_bf), kmean.reshape(bsz, n_qblk, BRANCH_W), block_onehot)
        ob = _sb_prompt(as3(qb), kt4(kbt_bf), as3(vb_bf), later_q2)
        yp = _merge(yp, oa.reshape(n_prompt, BRANCH_W), ob.reshape(n_prompt, BRANCH_W), sga, sgb, g,
                    wa_bf, wb_bf, wo_bf, name="merge_prompt")

        (qa_s, ka_s, va_s, qb_s, kb_s, vb_s, sga_s, sgb_s, g_s) = _proj(
            ys, *params, cos_s, s1_s, s2_s, seg, prompt=False)
        heads = lambda a: a.reshape(nb, N_HEADS, HEAD_DIM)
        ob_s, scores, sel_vec = _decode_stream(l, page_table, lane_replicated(qa_s), lane_replicated(qb_s),
                                               mk_t, sk_t, sv_t, later_p)
        sel_scalar = sel_vec[:, :, :, 0].reshape(nb, MOBA_TOPK * N_HEADS)
        oa_s = _moba_decode(l, page_table, sel_scalar, scores, sel_vec, heads(qa_s), heads(ka_s), heads(va_s), mv_t)
        ys = _merge(ys, oa_s.reshape(nb, BRANCH_W), ob_s.reshape(nb, BRANCH_W), sga_s, sgb_s, g_s,
                    wa_bf, wb_bf, wo_bf, name="merge_sample")
        for dst, src in zip(rows_s, (ka_s, va_s, kb_s, vb_s)):
            dst.append(src.reshape(nb, 1, N_HEADS, HEAD_DIM))

    rows_p = [jnp.transpose(a.reshape(depth, bsz, N_HEADS, HEAD_DIM, seq), (0, 1, 4, 2, 3)) for a in kv_prompt]
    return (yp.reshape(bsz, seq, D_MODEL), ys.reshape(nb, 1, D_MODEL),
            *rows_p, *(jnp.stack(r) for r in rows_s))
```

```python
import functools
import math

import jax
import jax.numpy as jnp
import numpy as np
from jax import lax
from jax.experimental import pallas as pl
from jax.experimental.pallas import tpu as pltpu

F32 = jnp.float32
BF16 = jnp.bfloat16

D_MODEL = 1024
HEAD_DIM = 64
N_HEADS = 8
BRANCH_W = N_HEADS * HEAD_DIM
ROT_DIM = HEAD_DIM // 4
ROPE_THETA = 500000.0
MOBA_BLOCK = 256
MOBA_TOPK = 3
EPS = 1e-6
NEG_INF = -1e30
QK_SCALE = 1.0 / math.sqrt(HEAD_DIM)
SB_DEAD_TAIL = 104.0

LANES = 128
HEADS_PER_TILE = LANES // HEAD_DIM
N_PAIRS = N_HEADS // HEADS_PER_TILE
Q_TILE = MOBA_BLOCK
MOBA_GROUP = 4
PROJ_ROWS = 256
MERGE_ROWS = 512
STREAM_PAGES = 16
VMEM_LIMIT = 56 * 1024 * 1024

_NT = (((1,), (1,)), ((), ()))


def _split3(x):
    a = x.astype(BF16)
    r = x - a.astype(F32)
    b = r.astype(BF16)
    c = (r - b.astype(F32)).astype(BF16)
    return a, b, c


def _split2(x):
    a = x.astype(BF16)
    b = (x - a.astype(F32)).astype(BF16)
    return a, b


def _sigmoid(x):
    return 1.0 / (1.0 + jnp.exp(-x))


def _neg_log_sigmoid_neg(z):
    return jnp.maximum(z, 0.0) + jnp.log(1.0 + jnp.exp(-jnp.abs(z)))


def _topk_rank(gate, axis_index):
    rank = jnp.zeros(gate.shape, F32)
    for jp in range(gate.shape[0]):
        row = gate[jp:jp + 1]
        rank = rank + jnp.where(row > gate, 1.0,
                                jnp.where(row == gate, jnp.where(jp < axis_index, 1.0, 0.0), 0.0))
    return rank


def _proj_kernel(prompt, n_carried, x_ref, ng_ref, w_ref, bg_ref, qg_ref, kg_ref, cos_ref, s1_ref, s2_ref,
                 seg_ref, *rest):
    outs = rest[n_carried:]
    x = x_ref[...]
    ms = jnp.mean(x * x, axis=-1, keepdims=True)
    h = (x * lax.rsqrt(ms + EPS) * ng_ref[...]).astype(BF16)

    def col(c):
        return jnp.dot(h, w_ref[:, c * BRANCH_W:(c + 1) * BRANCH_W], preferred_element_type=F32)

    def all_heads(table_ref):
        return jnp.concatenate([table_ref[...]] * N_PAIRS, axis=1)

    cos, s1, s2 = all_heads(cos_ref), all_heads(s1_ref), all_heads(s2_ref)

    def head_norm_rope(u, g):
        sq = (u * u).astype(BF16)
        half = seg_ref.shape[0]
        ss = jnp.concatenate([jnp.dot(sq[:, k:k + half], seg_ref[...], preferred_element_type=F32)
                              for k in range(0, BRANCH_W, half)], axis=1)
        y = u * lax.rsqrt(ss * (1.0 / HEAD_DIM) + EPS) * g
        return y * cos + pltpu.roll(y, BRANCH_W - ROT_DIM // 2, 1) * s1 + pltpu.roll(y, ROT_DIM // 2, 1) * s2

    def silu(u):
        return u * _sigmoid(u)

    qa = head_norm_rope(col(0), qg_ref[...]) * QK_SCALE
    ka = head_norm_rope(col(1), kg_ref[...])
    va = col(2)
    sga = silu(col(3))
    qb = col(4) * QK_SCALE
    kb = col(5)
    vb = col(6)
    sgb = silu(col(7))
    gm = jnp.dot(h, w_ref[:, 8 * BRANCH_W:], preferred_element_type=F32) + bg_ref[...]
    g = _sigmoid(gm)

    if prompt:
        (qa_ref, katbf_ref, vabf_ref, qb_ref, kbtbf_ref, vbbf_ref, kat_ref, vat_ref, kbt_ref, vbt_ref,
         sga_ref, sgb_ref, g_ref, kmean_ref) = outs
        qa_ref[...] = qa.astype(BF16)
        qb_ref[...] = qb.astype(BF16)
        vabf_ref[...] = va.astype(BF16)
        vbbf_ref[...] = vb.astype(BF16)
        ka_t, kb_t = ka.T, kb.T
        katbf_ref[0] = ka_t.astype(BF16)
        kbtbf_ref[0] = kb_t.astype(BF16)
        for ref, val in ((kat_ref, ka_t), (vat_ref, va.T), (kbt_ref, kb_t), (vbt_ref, vb.T)):
            for d in range(ref.shape[0]):
                ref[d, 0] = val
        kmean_ref[0] = jnp.mean(ka, axis=0, keepdims=True)
    else:
        qa_ref, ka_ref, va_ref, qb_ref, kb_ref, vb_ref, sga_ref, sgb_ref, g_ref = outs
        qa_ref[...] = qa
        qb_ref[...] = qb
        ka_ref[...] = ka
        va_ref[...] = va
        kb_ref[...] = kb
        vb_ref[...] = vb
    sga_ref[...] = sga.astype(sga_ref.dtype)
    sgb_ref[...] = sgb.astype(sgb_ref.dtype)
    g_ref[...] = g.astype(g_ref.dtype)


def _proj(x, ng, w_bf, bg, qg, kg, cos, s1, s2, seg, *, prompt, seq=None, layer=0, depth=1, carried=()):
    n = x.shape[0]
    tm = PROJ_ROWS if prompt else n
    nt = n // tm
    in_cols = w_bf.shape[1]
    row = lambda i: (i, 0)
    const = lambda i: (0, 0)
    wide = pl.BlockSpec((tm, BRANCH_W), row)
    tab_blocks = cos.shape[0] // tm
    table = pl.BlockSpec((tm, LANES), lambda i: (i % tab_blocks, 0))
    in_specs = [
        pl.BlockSpec((tm, D_MODEL), row),
        pl.BlockSpec((1, D_MODEL), const),
        pl.BlockSpec((D_MODEL, in_cols), const),
        pl.BlockSpec((1, 2 * D_MODEL), const),
        pl.BlockSpec((1, BRANCH_W), const),
        pl.BlockSpec((1, BRANCH_W), const),
        table, table, table,
        pl.BlockSpec(seg.shape, const),
    ]
    f32w = jax.ShapeDtypeStruct((n, BRANCH_W), F32)
    bf16w = jax.ShapeDtypeStruct((n, BRANCH_W), BF16)
    gate = jax.ShapeDtypeStruct((n, 2 * D_MODEL), BF16)
    gate_spec = pl.BlockSpec((tm, 2 * D_MODEL), row)
    if prompt:
        tiles_per_seq = seq // tm
        ktbf = jax.ShapeDtypeStruct((nt, BRANCH_W, tm), BF16)
        ktbf_spec = pl.BlockSpec((1, BRANCH_W, tm), lambda i: (i, 0, 0))
        kvt = jax.ShapeDtypeStruct((depth, n // seq, BRANCH_W, seq), F32)
        slabs = depth if layer == 0 else 1
        kvt_spec = pl.BlockSpec((slabs, 1, BRANCH_W, tm),
                                lambda i: (layer, i // tiles_per_seq, 0, i % tiles_per_seq))
        kmean = jax.ShapeDtypeStruct((nt, 1, BRANCH_W), F32)
        kmean_spec = pl.BlockSpec((1, 1, BRANCH_W), lambda i: (i, 0, 0))
        out_shape = (bf16w, ktbf, bf16w, bf16w, ktbf, bf16w, kvt, kvt, kvt, kvt, bf16w, bf16w, gate, kmean)
        out_specs = (wide, ktbf_spec, wide, wide, ktbf_spec, wide, kvt_spec, kvt_spec, kvt_spec, kvt_spec,
                     wide, wide, gate_spec, kmean_spec)
    else:
        out_shape = (f32w,) * 6 + (bf16w, bf16w, gate)
        out_specs = (wide,) * 8 + (gate_spec,)
    first_kv_out = 6
    aliases = {len(in_specs) + k: first_kv_out + k for k in range(len(carried))}
    in_specs += [pl.BlockSpec(memory_space=pl.ANY)] * len(carried)
    return pl.pallas_call(
        functools.partial(_proj_kernel, prompt, len(carried)),
        grid=(nt,),
        in_specs=in_specs,
        out_specs=out_specs,
        out_shape=out_shape,
        input_output_aliases=aliases,
        compiler_params=pltpu.CompilerParams(dimension_semantics=("arbitrary",), vmem_limit_bytes=VMEM_LIMIT),
        name="proj_prompt" if prompt else "proj_sample",
    )(x, ng, w_bf, bg, qg, kg, cos, s1, s2, seg, *carried)


def _head_masked(q, hh):
    lane = lax.broadcasted_iota(jnp.int32, (1, LANES), 1)
    keep = jnp.where((lane // HEAD_DIM) == hh, 1.0, 0.0)
    return (q.astype(F32) * keep).astype(BF16)


def _pick_head_lanes(per_head):
    lane = lax.broadcasted_iota(jnp.int32, (1, LANES), 1)
    out = per_head[-1]
    for hh in range(HEADS_PER_TILE - 2, -1, -1):
        out = jnp.where(lane < (hh + 1) * HEAD_DIM, per_head[hh], out)
    return out


def _attn_specs(b, t, nblk):
    in_specs = [
        pl.BlockSpec((1, Q_TILE, LANES), lambda bi, p, i: (bi, i, p)),
        pl.BlockSpec((1, nblk, LANES, Q_TILE), lambda bi, p, i: (bi, 0, p, 0)),
        pl.BlockSpec((1, t, LANES), lambda bi, p, i: (bi, 0, p)),
    ]
    out_spec = pl.BlockSpec((1, Q_TILE, LANES), lambda bi, p, i: (bi, i, p))
    return in_specs, out_spec


def _key_block(kt_ref, v_ref, j):
    return kt_ref[0, j], v_ref[0, pl.ds(pl.multiple_of(j * Q_TILE, Q_TILE), Q_TILE), :]


def _moba_kernel(q_ref, kt_ref, v_ref, km_ref, e_ref, o_ref, acc_ref):
    i = pl.program_id(2)
    nblk = km_ref.shape[1]
    q = q_ref[0]
    km_parts = _split3(km_ref[0])
    blk = lax.broadcasted_iota(jnp.int32, (nblk, Q_TILE), 0)
    past = blk < i
    r = lax.broadcasted_iota(jnp.int32, (Q_TILE, Q_TILE), 0)
    c = lax.broadcasted_iota(jnp.int32, (Q_TILE, Q_TILE), 1)
    causal = c <= r

    heads = range(HEADS_PER_TILE)
    lane = lax.broadcasted_iota(jnp.int32, (1, LANES), 1)
    in_head = [(lane // HEAD_DIM) == hh for hh in heads]

    def v_with_ones(v, hh):
        return jnp.where(jnp.broadcast_to(in_head[hh], v.shape), v, jnp.ones_like(v))

    kt_own, v_own = _key_block(kt_ref, v_ref, i)
    qms = [_head_masked(q, hh) for hh in heads]
    s_own = [jnp.dot(qm, kt_own, preferred_element_type=F32) for qm in qms]
    gates = [sum(lax.dot_general(p, qm, _NT, preferred_element_type=F32) for p in km_parts) for qm in qms]
    ms = []
    for hh in heads:
        s = jnp.where(causal, s_own[hh], NEG_INF)
        m = jnp.max(s, axis=-1, keepdims=True)
        acc_ref[hh] = jnp.dot(jnp.exp(s - m).astype(BF16), v_with_ones(v_own, hh), preferred_element_type=F32)
        ms.append(m)
    qh = []
    for hh in heads:
        rank = _topk_rank(jnp.where(past, gates[hh], NEG_INF), blk)
        pen_t = jnp.where(past, jnp.where(rank < MOBA_TOPK, 0.0, NEG_INF), NEG_INF)
        pen_t = jnp.concatenate([pen_t, jnp.zeros((LANES - nblk, Q_TILE), F32)], axis=0)
        qh.append(jnp.concatenate([qms[hh], pen_t.T.astype(BF16)], axis=1))

    def attend(j, width, ms):
        kts = [jnp.concatenate([kt_ref[0, j + w], e_ref[j + w]], axis=0) for w in range(width)]
        v = v_ref[0, pl.ds(pl.multiple_of(j * Q_TILE, Q_TILE), width * Q_TILE), :]
        scores = [[jnp.dot(qh[hh], kt, preferred_element_type=F32) for kt in kts] for hh in heads]
        out = []
        for hh in heads:
            s = scores[hh][0] if width == 1 else jnp.concatenate(scores[hh], axis=1)
            m_new = jnp.maximum(ms[hh], jnp.max(s, axis=-1, keepdims=True))
            p = jnp.exp(s - m_new)
            acc_ref[hh] = (jnp.exp(ms[hh] - m_new) * acc_ref[hh]
                           + jnp.dot(p.astype(BF16), v_with_ones(v, hh), preferred_element_type=F32))
            out.append(m_new)
        return tuple(out)

    ms = lax.fori_loop(0, i // MOBA_GROUP, lambda it, ms: attend(MOBA_GROUP * it, MOBA_GROUP, ms), tuple(ms))
    done = (i // MOBA_GROUP) * MOBA_GROUP
    width = MOBA_GROUP // 2
    while width >= 1:
        take = ((i - done) // width) % 2 == 1
        ms = lax.cond(take, functools.partial(attend, done, width), lambda ms: ms, ms)
        done = done + jnp.where(take, width, 0)
        width //= 2

    outs = []
    for hh in heads:
        acc = acc_ref[hh]
        row_sum = jnp.max(jnp.where(in_head[hh], 0.0, acc), axis=-1, keepdims=True)
        outs.append(acc / row_sum)
    o_ref[0] = _pick_head_lanes(outs).astype(o_ref.dtype)


def _moba_prompt(q_bf, kt_bf, v_bf, kmean, block_onehot):
    b, t, _ = q_bf.shape
    nblk = t // Q_TILE
    in_specs, out_spec = _attn_specs(b, t, nblk)
    in_specs += [
        pl.BlockSpec((1, nblk, LANES), lambda bi, p, i: (bi, 0, p)),
        pl.BlockSpec((nblk, LANES, Q_TILE), lambda bi, p, i: (0, 0, 0)),
    ]
    return pl.pallas_call(
        _moba_kernel,
        grid=(b, N_PAIRS, nblk),
        in_specs=in_specs,
        out_specs=out_spec,
        out_shape=jax.ShapeDtypeStruct((b, t, BRANCH_W), BF16),
        scratch_shapes=[pltpu.VMEM((HEADS_PER_TILE, Q_TILE, LANES), F32)],
        compiler_params=pltpu.CompilerParams(dimension_semantics=("arbitrary",) * 3, vmem_limit_bytes=VMEM_LIMIT),
        name="moba_prompt",
    )(q_bf, kt_bf, v_bf, kmean, block_onehot)


def _sb_kernel(q_ref, kt_ref, v_ref, u2_ref, o_ref, acc_ref):
    i = pl.program_id(2)
    q = q_ref[0]
    r = lax.broadcasted_iota(jnp.int32, (Q_TILE, Q_TILE), 0)
    c = lax.broadcasted_iota(jnp.int32, (Q_TILE, Q_TILE), 1)
    strict = c < r
    qh = [_head_masked(q, hh) for hh in range(HEADS_PER_TILE)]

    heads = range(HEADS_PER_TILE)

    def span(j, width, ends_with_own, tails):
        kts = [kt_ref[0, j + w] for w in range(width)]
        v = v_ref[0, pl.ds(pl.multiple_of(j * Q_TILE, Q_TILE), width * Q_TILE), :]
        zs = [[jnp.dot(qh[hh], kt, preferred_element_type=F32) for kt in kts] for hh in heads]
        out = []
        for hh in heads:
            seen = tails[hh]
            weights = [None] * width
            for w in reversed(range(width)):
                own = ends_with_own and w == width - 1
                z = zs[hh][w]
                nlm = _neg_log_sigmoid_neg(z)
                if own:
                    nlm = jnp.where(strict, nlm, 0.0)
                later = jnp.dot(jnp.concatenate(_split2(nlm), axis=1), u2_ref[...], preferred_element_type=F32)
                a = jnp.exp(z - nlm - (seen + later))
                if own:
                    a = jnp.where(strict, a, 0.0)
                weights[w] = a.astype(BF16)
                seen = seen + jnp.sum(nlm, axis=-1, keepdims=True)
            a_all = weights[0] if width == 1 else jnp.concatenate(weights, axis=1)
            acc_ref[hh] += jnp.dot(a_all, v, preferred_element_type=F32)
            out.append(seen)
        return tuple(out)

    acc_ref[...] = jnp.zeros_like(acc_ref)
    fresh = (jnp.zeros((Q_TILE, 1), F32),) * HEADS_PER_TILE
    tails = lax.cond(i >= 1, lambda: span(i - 1, 2, True, fresh), lambda: span(i, 1, True, fresh))

    def live(carry):
        j, tails = carry[0], carry[1:]
        lowest = functools.reduce(jnp.minimum, tails)
        return jnp.logical_and(j >= 0, jnp.min(lowest) < SB_DEAD_TAIL)

    def step(carry):
        j, tails = carry[0], carry[1:]
        return (j - 1,) + span(j, 1, False, tails)

    lax.while_loop(live, step, (i - 2,) + tails)
    o_ref[0] = _pick_head_lanes([acc_ref[hh] for hh in range(HEADS_PER_TILE)]).astype(o_ref.dtype)


def _sb_prompt(q_bf, kt_bf, v_bf, upper2):
    b, t, _ = q_bf.shape
    nblk = t // Q_TILE
    in_specs, out_spec = _attn_specs(b, t, nblk)
    in_specs.append(pl.BlockSpec((2 * Q_TILE, Q_TILE), lambda bi, p, i: (0, 0)))
    return pl.pallas_call(
        _sb_kernel,
        grid=(b, N_PAIRS, nblk),
        in_specs=in_specs,
        out_specs=out_spec,
        out_shape=jax.ShapeDtypeStruct((b, t, BRANCH_W), BF16),
        scratch_shapes=[pltpu.VMEM((HEADS_PER_TILE, Q_TILE, LANES), F32)],
        compiler_params=pltpu.CompilerParams(dimension_semantics=("arbitrary",) * 3, vmem_limit_bytes=VMEM_LIMIT),
        name="sb_prompt",
    )(q_bf, kt_bf, v_bf, upper2)


def _merge_kernel(x_ref, oa_ref, ob_ref, sga_ref, sgb_ref, g_ref, wa_ref, wb_ref, wo_ref, y_ref):
    f32 = lambda ref: ref[...].astype(F32)
    a = jnp.dot((f32(oa_ref) * f32(sga_ref)).astype(BF16), wa_ref[...], preferred_element_type=F32)
    b = jnp.dot((f32(ob_ref) * f32(sgb_ref)).astype(BF16), wb_ref[...], preferred_element_type=F32)
    g = f32(g_ref)
    merged = g[:, :D_MODEL] * a + g[:, D_MODEL:] * b
    y_ref[...] = x_ref[...] + jnp.dot(merged.astype(BF16), wo_ref[...], preferred_element_type=F32)


def _merge(x, oa, ob, sga, sgb, g, wa_bf, wb_bf, wo_bf, *, name):
    n = x.shape[0]
    tm = min(MERGE_ROWS, n)
    row = lambda i: (i, 0)
    const = lambda i: (0, 0)
    wide = pl.BlockSpec((tm, BRANCH_W), row)
    return pl.pallas_call(
        _merge_kernel,
        grid=(n // tm,),
        in_specs=[
            pl.BlockSpec((tm, D_MODEL), row), wide, wide, wide, wide,
            pl.BlockSpec((tm, 2 * D_MODEL), row),
            pl.BlockSpec((BRANCH_W, D_MODEL), const),
            pl.BlockSpec((BRANCH_W, D_MODEL), const),
            pl.BlockSpec((D_MODEL, D_MODEL), const),
        ],
        out_specs=pl.BlockSpec((tm, D_MODEL), row),
        out_shape=jax.ShapeDtypeStruct((n, D_MODEL), F32),
        compiler_params=pltpu.CompilerParams(dimension_semantics=("arbitrary",), vmem_limit_bytes=VMEM_LIMIT),
        name=name,
    )(x, oa, ob, sga, sgb, g, wa_bf, wb_bf, wo_bf)


def _stream_kernel(layer, nb, n_pages, pt_ref, qm_ref, qs_ref, mk_hbm, sk_hbm, sv_hbm, w_ref,
                   ob_ref, s_ref, sel_ref, mk_buf, sk_buf, sv_buf, sems, tail_ref, acc_ref):
    b = pl.program_id(0)
    c = pl.program_id(1)
    n_chunks = n_pages // STREAM_PAGES
    total = nb * n_chunks
    step = b * n_chunks + c
    slot = step % 2
    page_len = sk_buf.shape[-1]
    pages_per_block = MOBA_BLOCK // page_len
    n_blocks = n_pages // pages_per_block

    def copies(bb, cc, slot_):
        out = []
        for g in range(STREAM_PAGES):
            page = pt_ref[bb, n_pages - 1 - (cc * STREAM_PAGES + g)]
            for ci, (hbm, buf) in enumerate(((mk_hbm, mk_buf), (sk_hbm, sk_buf), (sv_hbm, sv_buf))):
                out.append(pltpu.make_async_copy(hbm.at[layer, page], buf.at[slot_, g], sems.at[slot_, ci]))
        return out

    @pl.when(step == 0)
    def _():
        for cp in copies(0, 0, 0):
            cp.start()

    @pl.when(step + 1 < total)
    def _():
        nxt = step + 1
        for cp in copies(nxt // n_chunks, nxt % n_chunks, 1 - slot):
            cp.start()

    for cp in copies(b, c, slot):
        cp.wait()

    @pl.when(c == 0)
    def _():
        tail_ref[...] = jnp.zeros_like(tail_ref)
        acc_ref[...] = jnp.zeros_like(acc_ref)

    def scores(k_page, q_rep):
        prod = (k_page * q_rep).reshape(N_HEADS, HEAD_DIM // 8, 8, page_len)
        return jnp.sum(jnp.sum(prod, axis=1), axis=1)

    pages = range(STREAM_PAGES)
    for g in pages:
        s_ref[0, n_pages - 1 - (c * STREAM_PAGES + g)] = scores(mk_buf[slot, g], qm_ref[0])

    z = jnp.stack([scores(sk_buf[slot, g], qs_ref[0]) for g in pages])
    nlm = _neg_log_sigmoid_neg(z)
    hi, lo = _split2(nlm.reshape(STREAM_PAGES * N_HEADS, page_len))
    later = (jnp.dot(hi, w_ref[...], preferred_element_type=F32)
             + jnp.dot(lo, w_ref[...], preferred_element_type=F32)).reshape(z.shape)
    page_sum = jnp.sum(nlm, axis=-1, keepdims=True)
    tails = [tail_ref[...]]
    for g in pages:
        tails.append(tails[-1] + page_sum[g])
    a = jnp.exp(z - nlm - (jnp.stack(tails[:-1]) + later))
    tail_ref[...] = tails[-1]
    acc = acc_ref[...]
    for g in pages:
        acc = acc + a[g][:, None, :] * sv_buf[slot, g]
    acc_ref[...] = acc

    @pl.when(c == n_chunks - 1)
    def _():
        ob_ref[0] = jnp.sum(acc_ref[...], axis=-1)
        scores = s_ref[0].reshape(n_blocks, pages_per_block, N_HEADS, page_len)
        gate = jnp.sum(jnp.sum(scores, axis=1), axis=-1, keepdims=True) * (1.0 / MOBA_BLOCK)
        gate = jnp.broadcast_to(gate, (n_blocks, N_HEADS, page_len))
        blk = lax.broadcasted_iota(jnp.int32, gate.shape, 0)
        rank = _topk_rank(gate, blk)
        for t in range(MOBA_TOPK):
            sel_ref[0, t] = jnp.sum(jnp.where(rank == float(t), blk.astype(F32), 0.0), axis=0).astype(jnp.int32)


def _decode_stream(layer, page_table, q_moba, q_sb, mk, sk, sv, lower):
    nb, n_pages = page_table.shape
    page_len = mk.shape[-1]
    n_chunks = n_pages // STREAM_PAGES
    any_spec = pl.BlockSpec(memory_space=pl.ANY)
    buf = pltpu.VMEM((2, STREAM_PAGES, N_HEADS, HEAD_DIM, page_len), F32)
    q_spec = pl.BlockSpec((1, N_HEADS, HEAD_DIM, page_len), lambda b, c, pt: (b, 0, 0, 0))
    grid_spec = pltpu.PrefetchScalarGridSpec(
        num_scalar_prefetch=1,
        grid=(nb, n_chunks),
        in_specs=[q_spec, q_spec, any_spec, any_spec, any_spec,
                  pl.BlockSpec((page_len, page_len), lambda b, c, pt: (0, 0))],
        out_specs=(
            pl.BlockSpec((1, N_HEADS, HEAD_DIM), lambda b, c, pt: (b, 0, 0)),
            pl.BlockSpec((1, n_pages, N_HEADS, page_len), lambda b, c, pt: (b, 0, 0, 0)),
            pl.BlockSpec((1, MOBA_TOPK, N_HEADS, page_len), lambda b, c, pt: (b, 0, 0, 0)),
        ),
        scratch_shapes=[buf, buf, buf, pltpu.SemaphoreType.DMA((2, 3)),
                        pltpu.VMEM((N_HEADS, 1), F32), pltpu.VMEM((N_HEADS, HEAD_DIM, page_len), F32)],
    )
    return pl.pallas_call(
        functools.partial(_stream_kernel, layer, nb, n_pages),
        grid_spec=grid_spec,
        out_shape=(jax.ShapeDtypeStruct((nb, N_HEADS, HEAD_DIM), F32),
                   jax.ShapeDtypeStruct((nb, n_pages, N_HEADS, page_len), F32),
                   jax.ShapeDtypeStruct((nb, MOBA_TOPK, N_HEADS, page_len), jnp.int32)),
        compiler_params=pltpu.CompilerParams(dimension_semantics=("arbitrary", "arbitrary"),
                                             vmem_limit_bytes=VMEM_LIMIT),
        name="decode_stream",
    )(page_table, q_moba, q_sb, mk, sk, sv, lower)


def _moba_decode_kernel(layer, nb, pt_ref, selsm_ref, s_ref, selv_ref, q_ref, kn_ref, vn_ref, mv_hbm, o_ref,
                        vbuf, sems, p_ref):
    b = pl.program_id(0)
    slot = b % 2
    n_pages, _, page_len = s_ref.shape[1:]
    pages_per_block = MOBA_BLOCK // page_len

    def tiles():
        return [(h, t, half) for h in range(N_HEADS) for t in range(MOBA_TOPK) for half in range(pages_per_block)]

    def copies(bb, slot_):
        out = []
        for h, t, half in tiles():
            page = pt_ref[bb, selsm_ref[bb, t * N_HEADS + h] * pages_per_block + half]
            out.append(pltpu.make_async_copy(mv_hbm.at[layer, page, h], vbuf.at[slot_, h, t, half], sems.at[slot_]))
        return out

    @pl.when(b == 0)
    def _():
        for cp in copies(0, 0):
            cp.start()

    @pl.when(b + 1 < nb)
    def _():
        for cp in copies(b + 1, 1 - slot):
            cp.start()

    s = s_ref[0]
    page_blk = lax.broadcasted_iota(jnp.int32, s.shape, 0) // pages_per_block
    sm = jnp.full(s.shape, NEG_INF, F32)
    for t in range(MOBA_TOPK):
        sm = jnp.where(page_blk == selv_ref[0, t][None], s, sm)
    s_new = jnp.sum(q_ref[0] * kn_ref[0], axis=-1, keepdims=True)
    m = jnp.maximum(jnp.max(jnp.max(sm, axis=0), axis=-1, keepdims=True), s_new)
    p = jnp.exp(sm - m)
    p_new = jnp.exp(s_new - m)
    l = jnp.sum(jnp.sum(p, axis=0), axis=-1, keepdims=True) + p_new
    p_ref[...] = p

    for cp in copies(b, slot):
        cp.wait()

    head = lax.broadcasted_iota(jnp.int32, (N_HEADS, HEAD_DIM), 0)
    acc = p_new * vn_ref[0]
    for h, t, half in tiles():
        page = selsm_ref[b, t * N_HEADS + h] * pages_per_block + half
        res = lax.dot_general(p_ref[page].astype(BF16), vbuf[slot, h, t, half].astype(BF16), _NT,
                              preferred_element_type=F32)
        acc = acc + jnp.where(head == h, res, 0.0)
    o_ref[0] = acc / l


def _moba_decode(layer, page_table, sel_scalar, scores, sel_vec, q, k_new, v_new, mv):
    nb, n_pages = page_table.shape
    page_len = mv.shape[-1]
    pages_per_block = MOBA_BLOCK // page_len
    any_spec = pl.BlockSpec(memory_space=pl.ANY)
    row = pl.BlockSpec((1, N_HEADS, HEAD_DIM), lambda b, pt, sl: (b, 0, 0))
    grid_spec = pltpu.PrefetchScalarGridSpec(
        num_scalar_prefetch=2,
        grid=(nb,),
        in_specs=[
            pl.BlockSpec((1, n_pages, N_HEADS, page_len), lambda b, pt, sl: (b, 0, 0, 0)),
            pl.BlockSpec((1, MOBA_TOPK, N_HEADS, page_len), lambda b, pt, sl: (b, 0, 0, 0)),
            row, row, row, any_spec,
        ],
        out_specs=row,
        scratch_shapes=[pltpu.VMEM((2, N_HEADS, MOBA_TOPK, pages_per_block, HEAD_DIM, page_len), F32),
                        pltpu.SemaphoreType.DMA((2,)),
                        pltpu.VMEM((n_pages, N_HEADS, page_len), F32)],
    )
    return pl.pallas_call(
        functools.partial(_moba_decode_kernel, layer, nb),
        grid_spec=grid_spec,
        out_shape=jax.ShapeDtypeStruct((nb, N_HEADS, HEAD_DIM), F32),
        compiler_params=pltpu.CompilerParams(dimension_semantics=("arbitrary",), vmem_limit_bytes=VMEM_LIMIT),
        name="moba_decode",
    )(page_table, sel_scalar, scores, sel_vec, q, k_new, v_new, mv)


def _rope_tables(pos):
    half = ROT_DIM // 2
    inv = ROPE_THETA ** (-jnp.arange(half, dtype=F32) * 2.0 / ROT_DIM)
    ang = pos[:, None] * inv[None, :]
    cos, sin = jnp.cos(ang), jnp.sin(ang)
    n = pos.shape[0]
    pad = jnp.zeros((n, HEAD_DIM - ROT_DIM), F32)
    zero = jnp.zeros((n, half), F32)
    cos_h = jnp.concatenate([cos, cos, pad + 1.0], axis=1)
    s1_h = jnp.concatenate([-sin, zero, pad], axis=1)
    s2_h = jnp.concatenate([zero, sin, pad], axis=1)
    return tuple(jnp.tile(t, (1, HEADS_PER_TILE)) for t in (cos_h, s1_h, s2_h))


def kernel(x_prompt, x_sample, cache_moba_k, cache_moba_v, cache_sb_k, cache_sb_v, page_table,
           norm_g, w_in, b_gate, q_norm_g, k_norm_g, w_branch_a, w_branch_b, w_out):
    depth = w_in.shape[0]
    bsz, seq, _ = x_prompt.shape
    nb, dec_seq, _ = x_sample.shape
    page_len = cache_moba_k.shape[2]
    n_pages = page_table.shape[1]
    past_len = n_pages * page_len
    n_prompt = bsz * seq
    n_qblk = seq // Q_TILE
    assert dec_seq == 1 and seq % Q_TILE == 0 and MOBA_BLOCK % page_len == 0 and n_qblk <= LANES
    assert n_pages % STREAM_PAGES == 0 and past_len % MOBA_BLOCK == 0 and page_len == LANES

    lane_head = np.arange(2 * LANES) // HEAD_DIM
    seg = (lane_head[:, None] == lane_head[None, :]).astype(BF16)
    idx = np.arange(Q_TILE)
    later_q = (idx[:, None] > idx[None, :]).astype(BF16)
    later_q2 = np.concatenate([later_q, later_q], axis=0)
    later_p = later_q[:page_len, :page_len]
    block_onehot = np.broadcast_to(
        (np.arange(LANES)[None, :, None] == np.arange(n_qblk)[:, None, None]), (n_qblk, LANES, Q_TILE)).astype(BF16)

    cos_p, s1_p, s2_p = _rope_tables(jnp.arange(seq, dtype=F32))
    cos_s, s1_s, s2_s = _rope_tables(jnp.full((nb,), past_len, F32))

    mk_t, mv_t, sk_t, sv_t = (jnp.transpose(c, (0, 1, 3, 4, 2))
                              for c in (cache_moba_k, cache_moba_v, cache_sb_k, cache_sb_v))

    def lane_replicated(q):
        return jnp.broadcast_to(q.reshape(nb, N_HEADS, HEAD_DIM, 1), (nb, N_HEADS, HEAD_DIM, page_len))

    yp = x_prompt.reshape(n_prompt, D_MODEL)
    ys = x_sample.reshape(nb, D_MODEL)
    kv_prompt = ()
    rows_s = [[] for _ in range(4)]
    for l in range(depth):
        w_bf = w_in[l].astype(BF16)
        wa_bf, wb_bf, wo_bf = (w[l].astype(BF16) for w in (w_branch_a, w_branch_b, w_out))
        params = (norm_g[l][None, :], w_bf, b_gate[l][None, :],
                  jnp.tile(q_norm_g[l], N_HEADS)[None, :], jnp.tile(k_norm_g[l], N_HEADS)[None, :])

        (qa, kat_bf, va_bf, qb, kbt_bf, vb_bf, *kv_prompt, sga, sgb, g, kmean) = _proj(
            yp, *params, cos_p, s1_p, s2_p, seg, prompt=True, seq=seq, layer=l, depth=depth, carried=kv_prompt)
        as3 = lambda a: a.reshape(bsz, seq, BRANCH_W)
        kt4 = lambda a: a.reshape(bsz, n_qblk, BRANCH_W, Q_TILE)
        oa = _moba_prompt(as3(qa), kt4(kat_bf), as3(va_bf), kmean.reshape(bsz, n_qblk, BRANCH_W), block_onehot)
        ob = _sb_prompt(as3(qb), kt4(kbt_bf), as3(vb_bf), later_q2)
        yp = _merge(yp, oa.reshape(n_prompt, BRANCH_W), ob.reshape(n_prompt, BRANCH_W), sga, sgb, g,
                    wa_bf, wb_bf, wo_bf, name="merge_prompt")

        (qa_s, ka_s, va_s, qb_s, kb_s, vb_s, sga_s, sgb_s, g_s) = _proj(
            ys, *params, cos_s, s1_s, s2_s, seg, prompt=False)
        heads = lambda a: a.reshape(nb, N_HEADS, HEAD_DIM)
        ob_s, scores, sel_vec = _decode_stream(l, page_table, lane_replicated(qa_s), lane_replicated(qb_s),
                                               mk_t, sk_t, sv_t, later_p)
        sel_scalar = sel_vec[:, :, :, 0].reshape(nb, MOBA_TOPK * N_HEADS)
        oa_s = _moba_decode(l, page_table, sel_scalar, scores, sel_vec, heads(qa_s), heads(ka_s), heads(va_s), mv_t)
        ys = _merge(ys, oa_s.reshape(nb, BRANCH_W), ob_s.reshape(nb, BRANCH_W), sga_s, sgb_s, g_s,
                    wa_bf, wb_bf, wo_bf, name="merge_sample")
        for dst, src in zip(rows_s, (ka_s, va_s, kb_s, vb_s)):
            dst.append(src.reshape(nb, 1, N_HEADS, HEAD_DIM))

    rows_p = [jnp.transpose(a.reshape(depth, bsz, N_HEADS, HEAD_DIM, seq), (0, 1, 4, 2, 3)) for a in kv_prompt]
    return (yp.reshape(bsz, seq, D_MODEL), ys.reshape(nb, 1, D_MODEL),
            *rows_p, *(jnp.stack(r) for r in rows_s))
```

```python
import functools
import math

import jax
import jax.numpy as jnp
import numpy as np
from jax import lax
from jax.experimental import pallas as pl
from jax.experimental.pallas import tpu as pltpu

F32 = jnp.float32
BF16 = jnp.bfloat16

D_MODEL = 1024
HEAD_DIM = 64
N_HEADS = 8
BRANCH_W = N_HEADS * HEAD_DIM
ROT_DIM = HEAD_DIM // 4
ROPE_THETA = 500000.0
MOBA_BLOCK = 256
MOBA_TOPK = 3
EPS = 1e-6
NEG_INF = -1e30
QK_SCALE = 1.0 / math.sqrt(HEAD_DIM)
SB_DEAD_TAIL = 104.0

LANES = 128
HEADS_PER_TILE = LANES // HEAD_DIM
N_PAIRS = N_HEADS // HEADS_PER_TILE
Q_TILE = MOBA_BLOCK
MOBA_GROUP = 4
MOBA_TILES = 4
SB_TILES = 2
PROJ_ROWS = 256
MERGE_ROWS = 512
STREAM_PAGES = 16
VMEM_LIMIT = 56 * 1024 * 1024

_NT = (((1,), (1,)), ((), ()))


def _split3(x):
    a = x.astype(BF16)
    r = x - a.astype(F32)
    b = r.astype(BF16)
    c = (r - b.astype(F32)).astype(BF16)
    return a, b, c


def _split2(x):
    a = x.astype(BF16)
    b = (x - a.astype(F32)).astype(BF16)
    return a, b


def _sigmoid(x):
    return 1.0 / (1.0 + jnp.exp(-x))


def _neg_log_sigmoid_neg(z):
    return jnp.maximum(z, 0.0) + jnp.log(1.0 + jnp.exp(-jnp.abs(z)))


def _topk_rank(gate, axis_index):
    rank = jnp.zeros(gate.shape, F32)
    for jp in range(gate.shape[0]):
        row = gate[jp:jp + 1]
        rank = rank + jnp.where(row > gate, 1.0,
                                jnp.where(row == gate, jnp.where(jp < axis_index, 1.0, 0.0), 0.0))
    return rank


def _proj_kernel(prompt, n_carried, x_ref, ng_ref, w_ref, bg_ref, qg_ref, kg_ref, cos_ref, s1_ref, s2_ref,
                 seg_ref, *rest):
    outs = rest[n_carried:]
    x = x_ref[...]
    ms = jnp.mean(x * x, axis=-1, keepdims=True)
    h = (x * lax.rsqrt(ms + EPS) * ng_ref[...]).astype(BF16)

    def col(c):
        return jnp.dot(h, w_ref[:, c * BRANCH_W:(c + 1) * BRANCH_W], preferred_element_type=F32)

    def all_heads(table_ref):
        return jnp.concatenate([table_ref[...]] * N_PAIRS, axis=1)

    cos, s1, s2 = all_heads(cos_ref), all_heads(s1_ref), all_heads(s2_ref)

    def head_norm_rope(u, g):
        sq = (u * u).astype(BF16)
        half = seg_ref.shape[0]
        ss = jnp.concatenate([jnp.dot(sq[:, k:k + half], seg_ref[...], preferred_element_type=F32)
                              for k in range(0, BRANCH_W, half)], axis=1)
        y = u * lax.rsqrt(ss * (1.0 / HEAD_DIM) + EPS) * g
        return y * cos + pltpu.roll(y, BRANCH_W - ROT_DIM // 2, 1) * s1 + pltpu.roll(y, ROT_DIM // 2, 1) * s2

    def silu(u):
        return u * _sigmoid(u)

    qa = head_norm_rope(col(0), qg_ref[...]) * QK_SCALE
    ka = head_norm_rope(col(1), kg_ref[...])
    va = col(2)
    sga = silu(col(3))
    qb = col(4) * QK_SCALE
    kb = col(5)
    vb = col(6)
    sgb = silu(col(7))
    gm = jnp.dot(h, w_ref[:, 8 * BRANCH_W:], preferred_element_type=F32) + bg_ref[...]
    g = _sigmoid(gm)

    if prompt:
        (qa_ref, katbf_ref, vabf_ref, qb_ref, kbtbf_ref, vbbf_ref, kat_ref, vat_ref, kbt_ref, vbt_ref,
         sga_ref, sgb_ref, g_ref, kmean_ref) = outs
        qa_ref[...] = qa.astype(BF16)
        qb_ref[...] = qb.astype(BF16)
        vabf_ref[...] = va.astype(BF16)
        vbbf_ref[...] = vb.astype(BF16)
        ka_t, kb_t = ka.T, kb.T
        katbf_ref[0] = ka_t.astype(BF16)
        kbtbf_ref[0] = kb_t.astype(BF16)
        for ref, val in ((kat_ref, ka_t), (vat_ref, va.T), (kbt_ref, kb_t), (vbt_ref, vb.T)):
            for d in range(ref.shape[0]):
                ref[d, 0] = val
        kmean_ref[0] = jnp.mean(ka, axis=0, keepdims=True)
    else:
        qa_ref, ka_ref, va_ref, qb_ref, kb_ref, vb_ref, sga_ref, sgb_ref, g_ref = outs
        qa_ref[...] = qa
        qb_ref[...] = qb
        ka_ref[...] = ka
        va_ref[...] = va
        kb_ref[...] = kb
        vb_ref[...] = vb
    sga_ref[...] = sga.astype(sga_ref.dtype)
    sgb_ref[...] = sgb.astype(sgb_ref.dtype)
    g_ref[...] = g.astype(g_ref.dtype)


def _proj(x, ng, w_bf, bg, qg, kg, cos, s1, s2, seg, *, prompt, seq=None, layer=0, depth=1, carried=()):
    n = x.shape[0]
    tm = PROJ_ROWS if prompt else n
    nt = n // tm
    in_cols = w_bf.shape[1]
    row = lambda i: (i, 0)
    const = lambda i: (0, 0)
    wide = pl.BlockSpec((tm, BRANCH_W), row)
    tab_blocks = cos.shape[0] // tm
    table = pl.BlockSpec((tm, LANES), lambda i: (i % tab_blocks, 0))
    in_specs = [
        pl.BlockSpec((tm, D_MODEL), row),
        pl.BlockSpec((1, D_MODEL), const),
        pl.BlockSpec((D_MODEL, in_cols), const),
        pl.BlockSpec((1, 2 * D_MODEL), const),
        pl.BlockSpec((1, BRANCH_W), const),
        pl.BlockSpec((1, BRANCH_W), const),
        table, table, table,
        pl.BlockSpec(seg.shape, const),
    ]
    f32w = jax.ShapeDtypeStruct((n, BRANCH_W), F32)
    bf16w = jax.ShapeDtypeStruct((n, BRANCH_W), BF16)
    gate = jax.ShapeDtypeStruct((n, 2 * D_MODEL), BF16)
    gate_spec = pl.BlockSpec((tm, 2 * D_MODEL), row)
    if prompt:
        tiles_per_seq = seq // tm
        ktbf = jax.ShapeDtypeStruct((nt, BRANCH_W, tm), BF16)
        ktbf_spec = pl.BlockSpec((1, BRANCH_W, tm), lambda i: (i, 0, 0))
        kvt = jax.ShapeDtypeStruct((depth, n // seq, BRANCH_W, seq), F32)
        slabs = depth if layer == 0 else 1
        kvt_spec = pl.BlockSpec((slabs, 1, BRANCH_W, tm),
                                lambda i: (layer, i // tiles_per_seq, 0, i % tiles_per_seq))
        kmean = jax.ShapeDtypeStruct((nt, 1, BRANCH_W), F32)
        kmean_spec = pl.BlockSpec((1, 1, BRANCH_W), lambda i: (i, 0, 0))
        out_shape = (bf16w, ktbf, bf16w, bf16w, ktbf, bf16w, kvt, kvt, kvt, kvt, bf16w, bf16w, gate, kmean)
        out_specs = (wide, ktbf_spec, wide, wide, ktbf_spec, wide, kvt_spec, kvt_spec, kvt_spec, kvt_spec,
                     wide, wide, gate_spec, kmean_spec)
    else:
        out_shape = (f32w,) * 6 + (bf16w, bf16w, gate)
        out_specs = (wide,) * 8 + (gate_spec,)
    first_kv_out = 6
    aliases = {len(in_specs) + k: first_kv_out + k for k in range(len(carried))}
    in_specs += [pl.BlockSpec(memory_space=pl.ANY)] * len(carried)
    return pl.pallas_call(
        functools.partial(_proj_kernel, prompt, len(carried)),
        grid=(nt,),
        in_specs=in_specs,
        out_specs=out_specs,
        out_shape=out_shape,
        input_output_aliases=aliases,
        compiler_params=pltpu.CompilerParams(dimension_semantics=("arbitrary",), vmem_limit_bytes=VMEM_LIMIT),
        name="proj_prompt" if prompt else "proj_sample",
    )(x, ng, w_bf, bg, qg, kg, cos, s1, s2, seg, *carried)


def _head_masked(q, hh):
    lane = lax.broadcasted_iota(jnp.int32, (1, LANES), 1)
    keep = jnp.where((lane // HEAD_DIM) == hh, 1.0, 0.0)
    return (q.astype(F32) * keep).astype(BF16)


def _pick_head_lanes(per_head):
    lane = lax.broadcasted_iota(jnp.int32, (1, LANES), 1)
    out = per_head[-1]
    for hh in range(HEADS_PER_TILE - 2, -1, -1):
        out = jnp.where(lane < (hh + 1) * HEAD_DIM, per_head[hh], out)
    return out


def _attn_specs(b, t, nblk, tiles=1):
    w = tiles * LANES
    in_specs = [
        pl.BlockSpec((1, Q_TILE, w), lambda bi, p, i: (bi, i, p)),
        pl.BlockSpec((1, nblk, w, Q_TILE), lambda bi, p, i: (bi, 0, p, 0)),
        pl.BlockSpec((1, t, w), lambda bi, p, i: (bi, 0, p)),
    ]
    out_spec = pl.BlockSpec((1, Q_TILE, w), lambda bi, p, i: (bi, i, p))
    return in_specs, out_spec


def _moba_kernel(q_ref, kt_ref, v_ref, km_ref, e_ref, o_ref, acc_ref):
    i = pl.program_id(2)
    nblk = km_ref.shape[1]
    tiles = q_ref.shape[2] // LANES
    blk = lax.broadcasted_iota(jnp.int32, (nblk, Q_TILE), 0)
    past = blk < i
    r = lax.broadcasted_iota(jnp.int32, (Q_TILE, Q_TILE), 0)
    c = lax.broadcasted_iota(jnp.int32, (Q_TILE, Q_TILE), 1)
    causal = c <= r

    heads = [(t, hh) for t in range(tiles) for hh in range(HEADS_PER_TILE)]
    lanes_of = lambda t: slice(t * LANES, (t + 1) * LANES)
    lane = lax.broadcasted_iota(jnp.int32, (1, LANES), 1)
    in_head = [(lane // HEAD_DIM) == hh for hh in range(HEADS_PER_TILE)]

    def v_with_ones(v, hh):
        return jnp.where(jnp.broadcast_to(in_head[hh], v.shape), v, jnp.ones_like(v))

    own_rows = pl.ds(pl.multiple_of(i * Q_TILE, Q_TILE), Q_TILE)
    qms = [_head_masked(q_ref[0, :, lanes_of(t)], hh) for t, hh in heads]
    s_own = [jnp.dot(qm, kt_ref[0, i, lanes_of(t), :], preferred_element_type=F32)
             for qm, (t, _) in zip(qms, heads)]
    km_parts = [_split3(km_ref[0, :, lanes_of(t)]) for t in range(tiles)]
    gates = [sum(lax.dot_general(p, qm, _NT, preferred_element_type=F32) for p in km_parts[t])
             for qm, (t, _) in zip(qms, heads)]
    ms = []
    for n, (t, hh) in enumerate(heads):
        s = jnp.where(causal, s_own[n], NEG_INF)
        m = jnp.max(s, axis=-1, keepdims=True)
        acc_ref[n] = jnp.dot(jnp.exp(s - m).astype(BF16), v_with_ones(v_ref[0, own_rows, lanes_of(t)], hh),
                             preferred_element_type=F32)
        ms.append(m)
    qh = []
    for n in range(len(heads)):
        rank = _topk_rank(jnp.where(past, gates[n], NEG_INF), blk)
        pen_t = jnp.where(past, jnp.where(rank < MOBA_TOPK, 0.0, NEG_INF), NEG_INF)
        pen_t = jnp.concatenate([pen_t, jnp.zeros((LANES - nblk, Q_TILE), F32)], axis=0)
        qh.append(jnp.concatenate([qms[n], pen_t.T.astype(BF16)], axis=1))

    def attend(j, width, ms):
        rows = pl.ds(pl.multiple_of(j * Q_TILE, Q_TILE), width * Q_TILE)
        kts = [[jnp.concatenate([kt_ref[0, j + w, lanes_of(t), :], e_ref[j + w]], axis=0) for w in range(width)]
               for t in range(tiles)]
        scores = [[jnp.dot(qh[n], kt, preferred_element_type=F32) for kt in kts[t]] for n, (t, _) in enumerate(heads)]
        out = []
        for n, (t, hh) in enumerate(heads):
            s = scores[n][0] if width == 1 else jnp.concatenate(scores[n], axis=1)
            m_new = jnp.maximum(ms[n], jnp.max(s, axis=-1, keepdims=True))
            p = jnp.exp(s - m_new)
            acc_ref[n] = (jnp.exp(ms[n] - m_new) * acc_ref[n]
                          + jnp.dot(p.astype(BF16), v_with_ones(v_ref[0, rows, lanes_of(t)], hh),
                                    preferred_element_type=F32))
            out.append(m_new)
        return tuple(out)

    ms = lax.fori_loop(0, i // MOBA_GROUP, lambda it, ms: attend(MOBA_GROUP * it, MOBA_GROUP, ms), tuple(ms))
    done = (i // MOBA_GROUP) * MOBA_GROUP
    width = MOBA_GROUP // 2
    while width >= 1:
        take = ((i - done) // width) % 2 == 1
        ms = lax.cond(take, functools.partial(attend, done, width), lambda ms: ms, ms)
        done = done + jnp.where(take, width, 0)
        width //= 2

    outs = []
    for n, (t, hh) in enumerate(heads):
        acc = acc_ref[n]
        row_sum = jnp.max(jnp.where(in_head[hh], 0.0, acc), axis=-1, keepdims=True)
        outs.append(acc / row_sum)
    for t in range(tiles):
        picked = _pick_head_lanes(outs[t * HEADS_PER_TILE:(t + 1) * HEADS_PER_TILE])
        o_ref[0, :, lanes_of(t)] = picked.astype(o_ref.dtype)


def _moba_prompt(q_bf, kt_bf, v_bf, kmean, block_onehot):
    b, t, _ = q_bf.shape
    nblk = t // Q_TILE
    in_specs, out_spec = _attn_specs(b, t, nblk, MOBA_TILES)
    in_specs += [
        pl.BlockSpec((1, nblk, MOBA_TILES * LANES), lambda bi, p, i: (bi, 0, p)),
        pl.BlockSpec((nblk, LANES, Q_TILE), lambda bi, p, i: (0, 0, 0)),
    ]
    return pl.pallas_call(
        _moba_kernel,
        grid=(b, N_PAIRS // MOBA_TILES, nblk),
        in_specs=in_specs,
        out_specs=out_spec,
        out_shape=jax.ShapeDtypeStruct((b, t, BRANCH_W), BF16),
        scratch_shapes=[pltpu.VMEM((MOBA_TILES * HEADS_PER_TILE, Q_TILE, LANES), F32)],
        compiler_params=pltpu.CompilerParams(dimension_semantics=("arbitrary",) * 3, vmem_limit_bytes=VMEM_LIMIT),
        name="moba_prompt",
    )(q_bf, kt_bf, v_bf, kmean, block_onehot)


def _sb_kernel(q_ref, kt_ref, v_ref, u2_ref, o_ref, acc_ref):
    i = pl.program_id(2)
    tiles = q_ref.shape[2] // LANES
    r = lax.broadcasted_iota(jnp.int32, (Q_TILE, Q_TILE), 0)
    c = lax.broadcasted_iota(jnp.int32, (Q_TILE, Q_TILE), 1)
    strict = c < r
    heads = [(t, hh) for t in range(tiles) for hh in range(HEADS_PER_TILE)]
    lanes_of = lambda t: slice(t * LANES, (t + 1) * LANES)
    qh = [_head_masked(q_ref[0, :, lanes_of(t)], hh) for t, hh in heads]

    def span(j, width, ends_with_own, tails):
        rows = pl.ds(pl.multiple_of(j * Q_TILE, Q_TILE), width * Q_TILE)
        zs = [[jnp.dot(qh[n], kt_ref[0, j + w, lanes_of(t), :], preferred_element_type=F32) for w in range(width)]
              for n, (t, _) in enumerate(heads)]
        out = []
        for n, (t, _) in enumerate(heads):
            seen = tails[n]
            weights = [None] * width
            for w in reversed(range(width)):
                own = ends_with_own and w == width - 1
                z = zs[n][w]
                nlm = _neg_log_sigmoid_neg(z)
                if own:
                    nlm = jnp.where(strict, nlm, 0.0)
                later = jnp.dot(jnp.concatenate(_split2(nlm), axis=1), u2_ref[...], preferred_element_type=F32)
                a = jnp.exp(z - nlm - (seen + later))
                if own:
                    a = jnp.where(strict, a, 0.0)
                weights[w] = a.astype(BF16)
                seen = seen + jnp.sum(nlm, axis=-1, keepdims=True)
            a_all = weights[0] if width == 1 else jnp.concatenate(weights, axis=1)
            acc_ref[n] += jnp.dot(a_all, v_ref[0, rows, lanes_of(t)], preferred_element_type=F32)
            out.append(seen)
        return tuple(out)

    acc_ref[...] = jnp.zeros_like(acc_ref)
    fresh = (jnp.zeros((Q_TILE, 1), F32),) * len(heads)
    tails = lax.cond(i >= 1, lambda: span(i - 1, 2, True, fresh), lambda: span(i, 1, True, fresh))

    def live(carry):
        j, tails = carry[0], carry[1:]
        lowest = functools.reduce(jnp.minimum, tails)
        return jnp.logical_and(j >= 0, jnp.min(lowest) < SB_DEAD_TAIL)

    def step(carry):
        j, tails = carry[0], carry[1:]
        return (j - 1,) + span(j, 1, False, tails)

    lax.while_loop(live, step, (i - 2,) + tails)
    for t in range(tiles):
        picked = _pick_head_lanes([acc_ref[t * HEADS_PER_TILE + hh] for hh in range(HEADS_PER_TILE)])
        o_ref[0, :, lanes_of(t)] = picked.astype(o_ref.dtype)


def _sb_prompt(q_bf, kt_bf, v_bf, upper2):
    b, t, _ = q_bf.shape
    nblk = t // Q_TILE
    in_specs, out_spec = _attn_specs(b, t, nblk, SB_TILES)
    in_specs.append(pl.BlockSpec((2 * Q_TILE, Q_TILE), lambda bi, p, i: (0, 0)))
    return pl.pallas_call(
        _sb_kernel,
        grid=(b, N_PAIRS // SB_TILES, nblk),
        in_specs=in_specs,
        out_specs=out_spec,
        out_shape=jax.ShapeDtypeStruct((b, t, BRANCH_W), BF16),
        scratch_shapes=[pltpu.VMEM((SB_TILES * HEADS_PER_TILE, Q_TILE, LANES), F32)],
        compiler_params=pltpu.CompilerParams(dimension_semantics=("arbitrary",) * 3, vmem_limit_bytes=VMEM_LIMIT),
        name="sb_prompt",
    )(q_bf, kt_bf, v_bf, upper2)


def _merge_kernel(x_ref, oa_ref, ob_ref, sga_ref, sgb_ref, g_ref, wa_ref, wb_ref, wo_ref, y_ref):
    f32 = lambda ref: ref[...].astype(F32)
    a = jnp.dot((f32(oa_ref) * f32(sga_ref)).astype(BF16), wa_ref[...], preferred_element_type=F32)
    b = jnp.dot((f32(ob_ref) * f32(sgb_ref)).astype(BF16), wb_ref[...], preferred_element_type=F32)
    g = f32(g_ref)
    merged = g[:, :D_MODEL] * a + g[:, D_MODEL:] * b
    y_ref[...] = x_ref[...] + jnp.dot(merged.astype(BF16), wo_ref[...], preferred_element_type=F32)


def _merge(x, oa, ob, sga, sgb, g, wa_bf, wb_bf, wo_bf, *, name):
    n = x.shape[0]
    tm = min(MERGE_ROWS, n)
    row = lambda i: (i, 0)
    const = lambda i: (0, 0)
    wide = pl.BlockSpec((tm, BRANCH_W), row)
    return pl.pallas_call(
        _merge_kernel,
        grid=(n // tm,),
        in_specs=[
            pl.BlockSpec((tm, D_MODEL), row), wide, wide, wide, wide,
            pl.BlockSpec((tm, 2 * D_MODEL), row),
            pl.BlockSpec((BRANCH_W, D_MODEL), const),
            pl.BlockSpec((BRANCH_W, D_MODEL), const),
            pl.BlockSpec((D_MODEL, D_MODEL), const),
        ],
        out_specs=pl.BlockSpec((tm, D_MODEL), row),
        out_shape=jax.ShapeDtypeStruct((n, D_MODEL), F32),
        compiler_params=pltpu.CompilerParams(dimension_semantics=("arbitrary",), vmem_limit_bytes=VMEM_LIMIT),
        name=name,
    )(x, oa, ob, sga, sgb, g, wa_bf, wb_bf, wo_bf)


def _stream_kernel(layer, nb, n_pages, pt_ref, qm_ref, qs_ref, mk_hbm, sk_hbm, sv_hbm, w_ref,
                   ob_ref, s_ref, sel_ref, mk_buf, sk_buf, sv_buf, sems, tail_ref, acc_ref):
    b = pl.program_id(0)
    c = pl.program_id(1)
    n_chunks = n_pages // STREAM_PAGES
    total = nb * n_chunks
    step = b * n_chunks + c
    slot = step % 2
    page_len = sk_buf.shape[-1]
    pages_per_block = MOBA_BLOCK // page_len
    n_blocks = n_pages // pages_per_block

    def copies(bb, cc, slot_):
        out = []
        for g in range(STREAM_PAGES):
            page = pt_ref[bb, n_pages - 1 - (cc * STREAM_PAGES + g)]
            for ci, (hbm, buf) in enumerate(((mk_hbm, mk_buf), (sk_hbm, sk_buf), (sv_hbm, sv_buf))):
                out.append(pltpu.make_async_copy(hbm.at[layer, page], buf.at[slot_, g], sems.at[slot_, ci]))
        return out

    @pl.when(step == 0)
    def _():
        for cp in copies(0, 0, 0):
            cp.start()

    @pl.when(step + 1 < total)
    def _():
        nxt = step + 1
        for cp in copies(nxt // n_chunks, nxt % n_chunks, 1 - slot):
            cp.start()

    for cp in copies(b, c, slot):
        cp.wait()

    @pl.when(c == 0)
    def _():
        tail_ref[...] = jnp.zeros_like(tail_ref)
        acc_ref[...] = jnp.zeros_like(acc_ref)

    def scores(k_page, q_rep):
        prod = (k_page * q_rep).reshape(N_HEADS, HEAD_DIM // 8, 8, page_len)
        return jnp.sum(jnp.sum(prod, axis=1), axis=1)

    pages = range(STREAM_PAGES)
    for g in pages:
        s_ref[0, n_pages - 1 - (c * STREAM_PAGES + g)] = scores(mk_buf[slot, g], qm_ref[0])

    z = jnp.stack([scores(sk_buf[slot, g], qs_ref[0]) for g in pages])
    nlm = _neg_log_sigmoid_neg(z)
    hi, lo = _split2(nlm.reshape(STREAM_PAGES * N_HEADS, page_len))
    later = (jnp.dot(hi, w_ref[...], preferred_element_type=F32)
             + jnp.dot(lo, w_ref[...], preferred_element_type=F32)).reshape(z.shape)
    page_sum = jnp.sum(nlm, axis=-1, keepdims=True)
    tails = [tail_ref[...]]
    for g in pages:
        tails.append(tails[-1] + page_sum[g])
    a = jnp.exp(z - nlm - (jnp.stack(tails[:-1]) + later))
    tail_ref[...] = tails[-1]
    acc = acc_ref[...]
    for g in pages:
        acc = acc + a[g][:, None, :] * sv_buf[slot, g]
    acc_ref[...] = acc

    @pl.when(c == n_chunks - 1)
    def _():
        ob_ref[0] = jnp.sum(acc_ref[...], axis=-1)
        scores = s_ref[0].reshape(n_blocks, pages_per_block, N_HEADS, page_len)
        gate = jnp.sum(jnp.sum(scores, axis=1), axis=-1, keepdims=True) * (1.0 / MOBA_BLOCK)
        gate = jnp.broadcast_to(gate, (n_blocks, N_HEADS, page_len))
        blk = lax.broadcasted_iota(jnp.int32, gate.shape, 0)
        rank = _topk_rank(gate, blk)
        for t in range(MOBA_TOPK):
            sel_ref[0, t] = jnp.sum(jnp.where(rank == float(t), blk.astype(F32), 0.0), axis=0).astype(jnp.int32)


def _decode_stream(layer, page_table, q_moba, q_sb, mk, sk, sv, lower):
    nb, n_pages = page_table.shape
    page_len = mk.shape[-1]
    n_chunks = n_pages // STREAM_PAGES
    any_spec = pl.BlockSpec(memory_space=pl.ANY)
    buf = pltpu.VMEM((2, STREAM_PAGES, N_HEADS, HEAD_DIM, page_len), F32)
    q_spec = pl.BlockSpec((1, N_HEADS, HEAD_DIM, page_len), lambda b, c, pt: (b, 0, 0, 0))
    grid_spec = pltpu.PrefetchScalarGridSpec(
        num_scalar_prefetch=1,
        grid=(nb, n_chunks),
        in_specs=[q_spec, q_spec, any_spec, any_spec, any_spec,
                  pl.BlockSpec((page_len, page_len), lambda b, c, pt: (0, 0))],
        out_specs=(
            pl.BlockSpec((1, N_HEADS, HEAD_DIM), lambda b, c, pt: (b, 0, 0)),
            pl.BlockSpec((1, n_pages, N_HEADS, page_len), lambda b, c, pt: (b, 0, 0, 0)),
            pl.BlockSpec((1, MOBA_TOPK, N_HEADS, page_len), lambda b, c, pt: (b, 0, 0, 0)),
        ),
        scratch_shapes=[buf, buf, buf, pltpu.SemaphoreType.DMA((2, 3)),
                        pltpu.VMEM((N_HEADS, 1), F32), pltpu.VMEM((N_HEADS, HEAD_DIM, page_len), F32)],
    )
    return pl.pallas_call(
        functools.partial(_stream_kernel, layer, nb, n_pages),
        grid_spec=grid_spec,
        out_shape=(jax.ShapeDtypeStruct((nb, N_HEADS, HEAD_DIM), F32),
                   jax.ShapeDtypeStruct((nb, n_pages, N_HEADS, page_len), F32),
                   jax.ShapeDtypeStruct((nb, MOBA_TOPK, N_HEADS, page_len), jnp.int32)),
        compiler_params=pltpu.CompilerParams(dimension_semantics=("arbitrary", "arbitrary"),
                                             vmem_limit_bytes=VMEM_LIMIT),
        name="decode_stream",
    )(page_table, q_moba, q_sb, mk, sk, sv, lower)


def _moba_decode_kernel(layer, nb, pt_ref, selsm_ref, s_ref, selv_ref, q_ref, kn_ref, vn_ref, mv_hbm, o_ref,
                        vbuf, sems, p_ref):
    b = pl.program_id(0)
    slot = b % 2
    n_pages, _, page_len = s_ref.shape[1:]
    pages_per_block = MOBA_BLOCK // page_len

    def tiles():
        return [(h, t, half) for h in range(N_HEADS) for t in range(MOBA_TOPK) for half in range(pages_per_block)]

    def copies(bb, slot_):
        out = []
        for h, t, half in tiles():
            page = pt_ref[bb, selsm_ref[bb, t * N_HEADS + h] * pages_per_block + half]
            out.append(pltpu.make_async_copy(mv_hbm.at[layer, page, h], vbuf.at[slot_, h, t, half], sems.at[slot_]))
        return out

    @pl.when(b == 0)
    def _():
        for cp in copies(0, 0):
            cp.start()

    @pl.when(b + 1 < nb)
    def _():
        for cp in copies(b + 1, 1 - slot):
            cp.start()

    s = s_ref[0]
    page_blk = lax.broadcasted_iota(jnp.int32, s.shape, 0) // pages_per_block
    sm = jnp.full(s.shape, NEG_INF, F32)
    for t in range(MOBA_TOPK):
        sm = jnp.where(page_blk == selv_ref[0, t][None], s, sm)
    s_new = jnp.sum(q_ref[0] * kn_ref[0], axis=-1, keepdims=True)
    m = jnp.maximum(jnp.max(jnp.max(sm, axis=0), axis=-1, keepdims=True), s_new)
    p = jnp.exp(sm - m)
    p_new = jnp.exp(s_new - m)
    l = jnp.sum(jnp.sum(p, axis=0), axis=-1, keepdims=True) + p_new
    p_ref[...] = p

    for cp in copies(b, slot):
        cp.wait()

    head = lax.broadcasted_iota(jnp.int32, (N_HEADS, HEAD_DIM), 0)
    acc = p_new * vn_ref[0]
    for h, t, half in tiles():
        page = selsm_ref[b, t * N_HEADS + h] * pages_per_block + half
        res = lax.dot_general(p_ref[page].astype(BF16), vbuf[slot, h, t, half].astype(BF16), _NT,
                              preferred_element_type=F32)
        acc = acc + jnp.where(head == h, res, 0.0)
    o_ref[0] = acc / l


def _moba_decode(layer, page_table, sel_scalar, scores, sel_vec, q, k_new, v_new, mv):
    nb, n_pages = page_table.shape
    page_len = mv.shape[-1]
    pages_per_block = MOBA_BLOCK // page_len
    any_spec = pl.BlockSpec(memory_space=pl.ANY)
    row = pl.BlockSpec((1, N_HEADS, HEAD_DIM), lambda b, pt, sl: (b, 0, 0))
    grid_spec = pltpu.PrefetchScalarGridSpec(
        num_scalar_prefetch=2,
        grid=(nb,),
        in_specs=[
            pl.BlockSpec((1, n_pages, N_HEADS, page_len), lambda b, pt, sl: (b, 0, 0, 0)),
            pl.BlockSpec((1, MOBA_TOPK, N_HEADS, page_len), lambda b, pt, sl: (b, 0, 0, 0)),
            row, row, row, any_spec,
        ],
        out_specs=row,
        scratch_shapes=[pltpu.VMEM((2, N_HEADS, MOBA_TOPK, pages_per_block, HEAD_DIM, page_len), F32),
                        pltpu.SemaphoreType.DMA((2,)),
                        pltpu.VMEM((n_pages, N_HEADS, page_len), F32)],
    )
    return pl.pallas_call(
        functools.partial(_moba_decode_kernel, layer, nb),
        grid_spec=grid_spec,
        out_shape=jax.ShapeDtypeStruct((nb, N_HEADS, HEAD_DIM), F32),
        compiler_params=pltpu.CompilerParams(dimension_semantics=("arbitrary",), vmem_limit_bytes=VMEM_LIMIT),
        name="moba_decode",
    )(page_table, sel_scalar, scores, sel_vec, q, k_new, v_new, mv)


def _rope_tables(pos):
    half = ROT_DIM // 2
    inv = ROPE_THETA ** (-jnp.arange(half, dtype=F32) * 2.0 / ROT_DIM)
    ang = pos[:, None] * inv[None, :]
    cos, sin = jnp.cos(ang), jnp.sin(ang)
    n = pos.shape[0]
    pad = jnp.zeros((n, HEAD_DIM - ROT_DIM), F32)
    zero = jnp.zeros((n, half), F32)
    cos_h = jnp.concatenate([cos, cos, pad + 1.0], axis=1)
    s1_h = jnp.concatenate([-sin, zero, pad], axis=1)
    s2_h = jnp.concatenate([zero, sin, pad], axis=1)
    return tuple(jnp.tile(t, (1, HEADS_PER_TILE)) for t in (cos_h, s1_h, s2_h))


def kernel(x_prompt, x_sample, cache_moba_k, cache_moba_v, cache_sb_k, cache_sb_v, page_table,
           norm_g, w_in, b_gate, q_norm_g, k_norm_g, w_branch_a, w_branch_b, w_out):
    depth = w_in.shape[0]
    bsz, seq, _ = x_prompt.shape
    nb, dec_seq, _ = x_sample.shape
    page_len = cache_moba_k.shape[2]
    n_pages = page_table.shape[1]
    past_len = n_pages * page_len
    n_prompt = bsz * seq
    n_qblk = seq // Q_TILE
    assert dec_seq == 1 and seq % Q_TILE == 0 and MOBA_BLOCK % page_len == 0 and n_qblk <= LANES
    assert n_pages % STREAM_PAGES == 0 and past_len % MOBA_BLOCK == 0 and page_len == LANES

    lane_head = np.arange(2 * LANES) // HEAD_DIM
    seg = (lane_head[:, None] == lane_head[None, :]).astype(BF16)
    idx = np.arange(Q_TILE)
    later_q = (idx[:, None] > idx[None, :]).astype(BF16)
    later_q2 = np.concatenate([later_q, later_q], axis=0)
    later_p = later_q[:page_len, :page_len]
    block_onehot = np.broadcast_to(
        (np.arange(LANES)[None, :, None] == np.arange(n_qblk)[:, None, None]), (n_qblk, LANES, Q_TILE)).astype(BF16)

    cos_p, s1_p, s2_p = _rope_tables(jnp.arange(seq, dtype=F32))
    cos_s, s1_s, s2_s = _rope_tables(jnp.full((nb,), past_len, F32))

    mk_t, mv_t, sk_t, sv_t = (jnp.transpose(c, (0, 1, 3, 4, 2))
                              for c in (cache_moba_k, cache_moba_v, cache_sb_k, cache_sb_v))

    def lane_replicated(q):
        return jnp.broadcast_to(q.reshape(nb, N_HEADS, HEAD_DIM, 1), (nb, N_HEADS, HEAD_DIM, page_len))

    yp = x_prompt.reshape(n_prompt, D_MODEL)
    ys = x_sample.reshape(nb, D_MODEL)
    kv_prompt = ()
    rows_s = [[] for _ in range(4)]
    for l in range(depth):
        w_bf = w_in[l].astype(BF16)
        wa_bf, wb_bf, wo_bf = (w[l].astype(BF16) for w in (w_branch_a, w_branch_b, w_out))
        params = (norm_g[l][None, :], w_bf, b_gate[l][None, :],
                  jnp.tile(q_norm_g[l], N_HEADS)[None, :], jnp.tile(k_norm_g[l], N_HEADS)[None, :])

        (qa, kat_bf, va_bf, qb, kbt_bf, vb_bf, *kv_prompt, sga, sgb, g, kmean) = _proj(
            yp, *params, cos_p, s1_p, s2_p, seg, prompt=True, seq=seq, layer=l, depth=depth, carried=kv_prompt)
        as3 = lambda a: a.reshape(bsz, seq, BRANCH_W)
        kt4 = lambda a: a.reshape(bsz, n_qblk, BRANCH_W, Q_TILE)
        oa = _moba_prompt(as3(qa), kt4(kat_bf), as3(va_bf), kmean.reshape(bsz, n_qblk, BRANCH_W), block_onehot)
        ob = _sb_prompt(as3(qb), kt4(kbt_bf), as3(vb_bf), later_q2)
        yp = _merge(yp, oa.reshape(n_prompt, BRANCH_W), ob.reshape(n_prompt, BRANCH_W), sga, sgb, g,
                    wa_bf, wb_bf, wo_bf, name="merge_prompt")

        (qa_s, ka_s, va_s, qb_s, kb_s, vb_s, sga_s, sgb_s, g_s) = _proj(
            ys, *params, cos_s, s1_s, s2_s, seg, prompt=False)
        heads = lambda a: a.reshape(nb, N_HEADS, HEAD_DIM)
        ob_s, scores, sel_vec = _decode_stream(l, page_table, lane_replicated(qa_s), lane_replicated(qb_s),
                                               mk_t, sk_t, sv_t, later_p)
        sel_scalar = sel_vec[:, :, :, 0].reshape(nb, MOBA_TOPK * N_HEADS)
        oa_s = _moba_decode(l, page_table, sel_scalar, scores, sel_vec, heads(qa_s), heads(ka_s), heads(va_s), mv_t)
        ys = _merge(ys, oa_s.reshape(nb, BRANCH_W), ob_s.reshape(nb, BRANCH_W), sga_s, sgb_s, g_s,
                    wa_bf, wb_bf, wo_bf, name="merge_sample")
        for dst, src in zip(rows_s, (ka_s, va_s, kb_s, vb_s)):
            dst.append(src.reshape(nb, 1, N_HEADS, HEAD_DIM))

    rows_p = [jnp.transpose(a.reshape(depth, bsz, N_HEADS, HEAD_DIM, seq), (0, 1, 4, 2, 3)) for a in kv_prompt]
    return (yp.reshape(bsz, seq, D_MODEL), ys.reshape(nb, 1, D_MODEL),
            *rows_p, *(jnp.stack(r) for r in rows_s))
```
